```python
import math
import jax, jax.numpy as jnp
from jax import lax
import numpy as np

D_MODEL = 1024
BATCH = 4
SEQ = 4096
DEPTH = 2
DEC_BATCH = 32
DEC_SEQ = 1
PAST_LEN = 16384
PAGE_SIZE = 128

N_HEADS = 16
HEAD_DIM = D_MODEL // N_HEADS
CHUNK = 128
D_GATE = D_MODEL
N_GROUPS = 8
GROUP_DIM = D_GATE // N_GROUPS
BLOCK = 256
TOPK = 3
Q_BLOCK = 32
N_BUCKETS = 32
MAX_DISTANCE = 128
D_FF = -(-8 * D_MODEL // (3 * 256)) * 256
N_A = DEPTH // 2
N_B = DEPTH - N_A
EPS = 1e-6
NEG = -1e30

kernel_name = "yoco_gmlp_moba_decoder_step"


def rmsnorm(x, g):
    xf = x.astype(jnp.float32)
    return (xf * lax.rsqrt(jnp.mean(xf * xf, -1, keepdims=True) + EPS)).astype(x.dtype) * g


def swiglu(x, w1, w3, w2):
    return (jax.nn.silu(x @ w1) * (x @ w3)) @ w2


def rel_bucket(rel):
    n = jnp.maximum(rel, 0)
    max_exact = N_BUCKETS // 2
    nf = jnp.maximum(n, 1).astype(jnp.float32)
    large = max_exact + (jnp.log(nf / max_exact) / math.log(MAX_DISTANCE / max_exact)
                         * (N_BUCKETS - max_exact)).astype(jnp.int32)
    large = jnp.minimum(large, N_BUCKETS - 1)
    return jnp.where(n < max_exact, n, large)


def gmlp_mixer(x, w_in, g_v, w_s, b_s, w_out):
    B, S, _ = x.shape
    s_pad = -(-S // CHUNK) * CHUNK
    uv = jax.nn.gelu(x @ w_in)
    u, v = uv[..., :D_GATE], uv[..., D_GATE:]
    v = rmsnorm(v, g_v)
    vp = jnp.pad(v, ((0, 0), (0, s_pad - S), (0, 0))).reshape(B, s_pad // CHUNK, CHUNK, N_GROUPS, GROUP_DIM)
    w_causal = jnp.tril(w_s)
    mixed = jnp.einsum('gij,bnjgc->bnigc', w_causal, vp) + b_s.T[:, :, None]
    mixed = mixed.reshape(B, s_pad, D_GATE)[:, :S]
    return (u * mixed) @ w_out, v


def shared_kv(h, g_kv, w_k, w_v, g_k):
    c = rmsnorm(h, g_kv)
    B, S, _ = c.shape
    k = rmsnorm((c @ w_k).reshape(B, S, N_HEADS, HEAD_DIM), g_k).transpose(0, 2, 1, 3)
    v = (c @ w_v).reshape(B, S, N_HEADS, HEAD_DIM).transpose(0, 2, 1, 3)
    return k, v


def moba_core(q, q_pos, means, gather_fn, own_k, own_v, own_pos, rel_table):
    B, H, T, _ = q.shape
    scale = HEAD_DIM ** -0.5
    q_blk = q_pos // BLOCK
    nc = means.shape[2]
    gate = jnp.einsum('bhtd,bhnd->bhtn', q.astype(jnp.float32), means.astype(jnp.float32))
    gate = jnp.where(jnp.arange(nc)[None, :] < q_blk[:, None], gate, NEG)
    _, idx = lax.top_k(gate, TOPK)
    sel_ok = idx < q_blk[:, None]
    k_sel, v_sel = gather_fn(idx)
    k_pos = idx[..., None] * BLOCK + jnp.arange(BLOCK)
    table_h = rel_table.T.astype(jnp.float32)
    h_ix = jnp.arange(H)[None, :, None, None, None]
    s_sel = (jnp.einsum('bhtd,bhtkjd->bhtkj', q, k_sel).astype(jnp.float32) * scale
             + table_h[h_ix, rel_bucket(q_pos[:, None, None] - k_pos)])
    s_sel = jnp.where(sel_ok[..., None], s_sel, NEG).reshape(B, H, T, TOPK * BLOCK)
    rel_own = q_pos[:, None] - own_pos[None, :]
    s_own = (jnp.einsum('bhtd,bhld->bhtl', q, own_k).astype(jnp.float32) * scale
             + table_h[:, rel_bucket(rel_own)][None])
    own_ok = (own_pos[None, :] // BLOCK == q_blk[:, None]) & (rel_own >= 0)
    s_own = jnp.where(own_ok, s_own, NEG)
    p = jax.nn.softmax(jnp.concatenate([s_sel, s_own], -1), axis=-1).astype(q.dtype)
    p_sel = p[..., :TOPK * BLOCK].reshape(B, H, T, TOPK, BLOCK)
    p_own = p[..., TOPK * BLOCK:]
    return (jnp.einsum('bhtkj,bhtkjd->bhtd', p_sel, v_sel)
            + jnp.einsum('bhtl,bhld->bhtd', p_own, own_v))


def moba_prompt(q, k, v, rel_table):
    B, H, S, D = q.shape
    nb = -(-S // BLOCK)
    pad = nb * BLOCK - S
    kp = jnp.pad(k, ((0, 0), (0, 0), (0, pad), (0, 0)))
    vp = jnp.pad(v, ((0, 0), (0, 0), (0, pad), (0, 0)))
    kb = kp.reshape(B, H, nb, BLOCK, D)
    vb = vp.reshape(B, H, nb, BLOCK, D)
    means = kb.astype(jnp.float32).mean(axis=3)
    means = jnp.pad(means, ((0, 0), (0, 0), (0, max(TOPK - nb, 0)), (0, 0)))
    b_ix = jnp.arange(B)[:, None, None, None]
    h_ix = jnp.arange(H)[None, :, None, None]

    def gather_fn(idx):
        idx_c = jnp.minimum(idx, nb - 1)
        return kb[b_ix, h_ix, idx_c], vb[b_ix, h_ix, idx_c]

    def one_block(i):
        qs = i * Q_BLOCK
        q_i = lax.dynamic_slice_in_dim(q, qs, Q_BLOCK, axis=2)
        own_start = (qs // BLOCK) * BLOCK
        own_k = lax.dynamic_slice_in_dim(kp, own_start, BLOCK, axis=2)
        own_v = lax.dynamic_slice_in_dim(vp, own_start, BLOCK, axis=2)
        q_pos = qs + jnp.arange(Q_BLOCK)
        own_pos = own_start + jnp.arange(BLOCK)
        return moba_core(q_i, q_pos, means, gather_fn, own_k, own_v, own_pos, rel_table)

    out = lax.map(one_block, jnp.arange(S // Q_BLOCK))
    return out.transpose(1, 2, 0, 3, 4).reshape(B, H, S, D)


def moba_sample(q, k_new, v_new, cache_k, cache_v, page_table, rel_table):
    B, H, T, D = q.shape
    n_pages = page_table.shape[1]
    past = n_pages * PAGE_SIZE
    ppb = BLOCK // PAGE_SIZE
    nbc = past // BLOCK
    tail_pages = n_pages - nbc * ppb
    page_mean = cache_k.astype(jnp.float32).mean(axis=2)
    pm = page_mean[page_table[:, :nbc * ppb]]
    cmeans = pm.reshape(B, nbc, ppb, H, D).mean(axis=2).transpose(0, 2, 1, 3)
    tail_phys = page_table[:, nbc * ppb:]
    tail_k = cache_k[tail_phys].transpose(0, 2, 1, 3, 4).reshape(B, H, tail_pages * PAGE_SIZE, D)
    tail_v = cache_v[tail_phys].transpose(0, 2, 1, 3, 4).reshape(B, H, tail_pages * PAGE_SIZE, D)
    loc_k = jnp.concatenate([tail_k, k_new], axis=2)
    loc_v = jnp.concatenate([tail_v, v_new], axis=2)
    L = loc_k.shape[2]
    nbl = -(-L // BLOCK)
    lpad = nbl * BLOCK - L
    loc_kb = jnp.pad(loc_k, ((0, 0), (0, 0), (0, lpad), (0, 0))).reshape(B, H, nbl, BLOCK, D)
    loc_vb = jnp.pad(loc_v, ((0, 0), (0, 0), (0, lpad), (0, 0))).reshape(B, H, nbl, BLOCK, D)
    lmeans = loc_kb.astype(jnp.float32).mean(axis=3)
    means = jnp.concatenate([cmeans, lmeans], axis=2)
    means = jnp.pad(means, ((0, 0), (0, 0), (0, max(TOPK - nbc - nbl, 0)), (0, 0)))
    b4 = jnp.arange(B)[:, None, None, None]
    h4 = jnp.arange(H)[None, :, None, None]
    b5 = b4[..., None]
    h5 = h4[..., None]

    def gather_fn(idx):
        from_cache = (idx < nbc)[..., None, None]
        logical = jnp.minimum(idx[..., None] * ppb + jnp.arange(ppb), n_pages - 1)
        phys = page_table[b5, logical]
        kc = cache_k[phys, h5].reshape(B, H, T, TOPK, BLOCK, D)
        vc = cache_v[phys, h5].reshape(B, H, T, TOPK, BLOCK, D)
        li = jnp.clip(idx - nbc, 0, nbl - 1)
        return (jnp.where(from_cache, kc, loc_kb[b4, h4, li]),
                jnp.where(from_cache, vc, loc_vb[b4, h4, li]))

    q_pos = past + jnp.arange(T)
    own_pos = nbc * BLOCK + jnp.arange(L)
    return moba_core(q, q_pos, means, gather_fn, loc_k, loc_v, own_pos, rel_table)


def run_trunk(x, moba_fn, g_mix, g_ffn, w_gin, g_gv, w_sp, b_sp, w_gout,
              g_kv, w_k, w_v, g_k, w_q, g_q, w_o, w_f1, w_f3, w_f2):
    h = x
    B, S, _ = x.shape
    v_rows = []
    k = v = None
    for layer in range(DEPTH):
        hn = rmsnorm(h, g_mix[layer])
        if layer < N_A:
            out, vr = gmlp_mixer(hn, w_gin[layer], g_gv[layer], w_sp[layer], b_sp[layer], w_gout[layer])
            v_rows.append(vr)
        else:
            j = layer - N_A
            q = rmsnorm((hn @ w_q[j]).reshape(B, S, N_HEADS, HEAD_DIM), g_q[j]).transpose(0, 2, 1, 3)
            o = moba_fn(q, k, v)
            out = o.transpose(0, 2, 1, 3).reshape(B, S, D_MODEL) @ w_o[j]
        h = h + out
        h = h + swiglu(rmsnorm(h, g_ffn[layer]), w_f1[layer], w_f3[layer], w_f2[layer])
        if layer == N_A - 1:
            k, v = shared_kv(h, g_kv, w_k, w_v, g_k)
    return h, k, v, v_rows


def setup_inputs(seed: int = 0) -> dict:
    key = jax.random.key(seed)
    ks = jax.random.split(key, 32)
    f32 = jnp.float32
    n_pages = PAST_LEN // PAGE_SIZE
    n_pool = (DEC_BATCH * n_pages * 5) // 4

    def nrm(k, shape, s):
        return jax.random.normal(k, shape, f32) * s

    def gain(k, shape):
        return 1.0 + 0.05 * jax.random.normal(k, shape, f32)

    page_table = jax.random.permutation(ks[4], n_pool)[:DEC_BATCH * n_pages].reshape(DEC_BATCH, n_pages).astype(jnp.int32)
    return {
        "x_prompt": nrm(ks[0], (BATCH, SEQ, D_MODEL), 1.0),
        "x_sample": nrm(ks[1], (DEC_BATCH, DEC_SEQ, D_MODEL), 1.0),
        "cache_k": nrm(ks[2], (n_pool, N_HEADS, PAGE_SIZE, HEAD_DIM), 1.0),
        "cache_v": nrm(ks[3], (n_pool, N_HEADS, PAGE_SIZE, HEAD_DIM), 1.0),
        "page_table": page_table,
        "rel_table": nrm(ks[5], (N_BUCKETS, N_HEADS), 0.5),
        "g_mix": gain(ks[6], (DEPTH, D_MODEL)),
        "g_ffn": gain(ks[7], (DEPTH, D_MODEL)),
        "w_gin": nrm(ks[8], (N_A, D_MODEL, 2 * D_GATE), D_MODEL ** -0.5),
        "g_gv": gain(ks[9], (N_A, D_GATE)),
        "w_sp": nrm(ks[10], (N_A, N_GROUPS, CHUNK, CHUNK), CHUNK ** -0.5),
        "b_sp": gain(ks[11], (N_A, N_GROUPS, CHUNK)),
        "w_gout": nrm(ks[12], (N_A, D_GATE, D_MODEL), D_GATE ** -0.5),
        "g_kv": gain(ks[13], (D_MODEL,)),
        "w_k": nrm(ks[14], (D_MODEL, N_HEADS * HEAD_DIM), D_MODEL ** -0.5),
        "w_v": nrm(ks[15], (D_MODEL, N_HEADS * HEAD_DIM), D_MODEL ** -0.5),
        "g_k": gain(ks[16], (HEAD_DIM,)),
        "w_q": nrm(ks[17], (N_B, D_MODEL, N_HEADS * HEAD_DIM), D_MODEL ** -0.5),
        "g_q": gain(ks[18], (N_B, HEAD_DIM)),
        "w_o": nrm(ks[19], (N_B, N_HEADS * HEAD_DIM, D_MODEL), D_MODEL ** -0.5),
        "w_f1": nrm(ks[20], (DEPTH, D_MODEL, D_FF), D_MODEL ** -0.5),
        "w_f3": nrm(ks[21], (DEPTH, D_MODEL, D_FF), D_MODEL ** -0.5),
        "w_f2": nrm(ks[22], (DEPTH, D_FF, D_MODEL), D_FF ** -0.5),
    }


def reference(x_prompt, x_sample, cache_k, cache_v, page_table, rel_table,
              g_mix, g_ffn, w_gin, g_gv, w_sp, b_sp, w_gout,
              g_kv, w_k, w_v, g_k, w_q, g_q, w_o, w_f1, w_f3, w_f2):
    weights = (g_mix, g_ffn, w_gin, g_gv, w_sp, b_sp, w_gout,
               g_kv, w_k, w_v, g_k, w_q, g_q, w_o, w_f1, w_f3, w_f2)
    y_prompt, prompt_k, prompt_v, _ = run_trunk(
        x_prompt, lambda q, k, v: moba_prompt(q, k, v, rel_table), *weights)
    y_sample, sample_k, sample_v, s_rows = run_trunk(
        x_sample, lambda q, k, v: moba_sample(q, k, v, cache_k, cache_v, page_table, rel_table), *weights)
    sample_gmlp_v = jnp.stack(s_rows)
    return (y_prompt, y_sample, prompt_k, prompt_v, sample_k, sample_v, sample_gmlp_v)
```

```python
import functools
import math

import numpy as np
import jax
import jax.numpy as jnp
from jax import lax
from jax.experimental import pallas as pl
from jax.experimental.pallas import tpu as pltpu

F32 = jnp.float32
BF16 = jnp.bfloat16

D_MODEL = 1024
N_HEADS = 16
HEAD_DIM = 64
CHUNK = 128
N_GROUPS = 8
GROUP_DIM = 128
D_GATE = 1024
BLOCK = 256
TOPK = 3
N_BUCKETS = 32
MAX_DISTANCE = 128
PAGE_SIZE = 128
PAGES_PER_BLOCK = BLOCK // PAGE_SIZE
EPS = 1e-6
NEG = -1e30
SCALE = HEAD_DIM ** -0.5

V7X_LANES = 128
V7X_MXU_DIM = 256
V7X_VMEM_LIMIT = 56 * 1024 * 1024

FFN_CHUNK = V7X_MXU_DIM
HEADS_PER_LANE_TILE = V7X_LANES // HEAD_DIM
NORM_COLS = V7X_MXU_DIM
MEAN_PAGES_PER_STEP = 8


def _rms(x):
    return x * lax.rsqrt(jnp.mean(x * x, axis=-1, keepdims=True) + EPS)


def _gelu_tanh(x):
    c = math.sqrt(2.0 / math.pi)
    return x * (0.5 * (1.0 + jnp.tanh(c * (x + 0.044715 * (x * x * x)))))


def _dot(a, b):
    return jnp.dot(a, b, preferred_element_type=F32)


def _params(sem, vmem=V7X_VMEM_LIMIT):
    return pltpu.CompilerParams(dimension_semantics=sem, vmem_limit_bytes=vmem)


def _const_spec(shape):
    zeros = (0,) * len(shape)
    return pl.BlockSpec(shape, lambda *_: zeros, pipeline_mode=pl.Buffered(1))


def _gmlp_body(x_ref, gmix_ref, win_ref, ggv_ref, wsp_ref, bsp_ref, wout_ref, *rest, single_token):
    x = x_ref[...]
    hn = (_rms(x) * gmix_ref[...]).astype(BF16)
    uv = _gelu_tanh(_dot(hn, win_ref[...]))
    u = uv[:, :D_GATE]
    v = _rms(uv[:, D_GATE:]) * ggv_ref[...]
    if single_token:
        h_ref, v_ref = rest
        v_ref[...] = v
        mixed = v * wsp_ref[...] + bsp_ref[...]
    else:
        h_ref, mixed_ref = rest
        vb = v.astype(BF16)
        row = lax.broadcasted_iota(jnp.int32, (CHUNK, CHUNK), 0)
        col = lax.broadcasted_iota(jnp.int32, (CHUNK, CHUNK), 1)
        for g in range(N_GROUPS):
            cols = slice(g * GROUP_DIM, (g + 1) * GROUP_DIM)
            w = jnp.where(row >= col, wsp_ref[g], 0.0).astype(BF16)
            for n in range(x.shape[0] // CHUNK):
                rows = slice(n * CHUNK, (n + 1) * CHUNK)
                mixed_ref[rows, cols] = _dot(w, vb[rows, cols]) + bsp_ref[:, cols]
        mixed = mixed_ref[...]
    h_ref[...] = x + _dot((u * mixed).astype(BF16), wout_ref[...])


def _gmlp(x, g_mix, w_in, g_gv, w_sp, b_sp, w_out, *, single_token, tm):
    m = x.shape[0]
    row_spec = pl.BlockSpec((tm, D_MODEL), lambda i: (i, 0))
    if single_token:
        wsp_arg = jnp.repeat(w_sp[:, 0, 0], GROUP_DIM)[None, :]
        bsp_arg = jnp.repeat(b_sp[:, 0], GROUP_DIM)[None, :]
        wsp_spec = _const_spec((1, D_GATE))
        bsp_spec = _const_spec((1, D_GATE))
        out_shape = (jax.ShapeDtypeStruct((m, D_MODEL), F32), jax.ShapeDtypeStruct((m, D_GATE), F32))
        out_specs = (row_spec, pl.BlockSpec((tm, D_GATE), lambda i: (i, 0)))
        scratch = []
    else:
        wsp_arg = w_sp
        bsp_arg = jnp.repeat(b_sp.T, GROUP_DIM, axis=1)
        wsp_spec = _const_spec((N_GROUPS, CHUNK, CHUNK))
        bsp_spec = _const_spec((CHUNK, D_GATE))
        out_shape = jax.ShapeDtypeStruct((m, D_MODEL), F32)
        out_specs = row_spec
        scratch = [pltpu.VMEM((tm, D_GATE), F32)]
    return pl.pallas_call(
        functools.partial(_gmlp_body, single_token=single_token),
        grid=(m // tm,),
        in_specs=[row_spec, _const_spec((1, D_MODEL)), _const_spec((D_MODEL, 2 * D_GATE)),
                  _const_spec((1, D_GATE)), wsp_spec, bsp_spec, _const_spec((D_GATE, D_MODEL))],
        out_specs=out_specs,
        out_shape=out_shape,
        scratch_shapes=scratch,
        compiler_params=_params(("parallel",)),
        name="gmlp_decode" if single_token else "gmlp",
    )(x, g_mix[None, :], w_in, g_gv[None, :], wsp_arg, bsp_arg, w_out)


def _ffn_body(*refs, has_proj):
    if has_proj:
        h_ref, a_ref, wo_ref, g_ref, w1_ref, w3_ref, w2_ref, o_ref = refs
        h = h_ref[...] + _dot(a_ref[...], wo_ref[...])
    else:
        h_ref, g_ref, w1_ref, w3_ref, w2_ref, o_ref = refs
        h = h_ref[...]
    n = (_rms(h) * g_ref[...]).astype(BF16)
    acc = h
    for c in range(w1_ref.shape[1] // FFN_CHUNK):
        cols = slice(c * FFN_CHUNK, (c + 1) * FFN_CHUNK)
        a = _dot(n, w1_ref[:, cols])
        b = _dot(n, w3_ref[:, cols])
        acc = acc + _dot((a * jax.nn.sigmoid(a) * b).astype(BF16), w2_ref[cols, :])
    o_ref[...] = acc


def _ffn(h, g, w1, w3, w2, *, tm, attn=None, w_o=None):
    m = h.shape[0]
    d_ff = w1.shape[1]
    row_spec = pl.BlockSpec((tm, D_MODEL), lambda i: (i, 0))
    has_proj = attn is not None
    args = [h]
    specs = [row_spec]
    if has_proj:
        args += [attn, w_o]
        specs += [row_spec, _const_spec((D_MODEL, D_MODEL))]
    args += [g[None, :], w1, w3, w2]
    specs += [_const_spec((1, D_MODEL)), _const_spec((D_MODEL, d_ff)), _const_spec((D_MODEL, d_ff)),
              _const_spec((d_ff, D_MODEL))]
    return pl.pallas_call(
        functools.partial(_ffn_body, has_proj=has_proj),
        grid=(m // tm,),
        in_specs=specs,
        out_specs=row_spec,
        out_shape=jax.ShapeDtypeStruct((m, D_MODEL), F32),
        compiler_params=_params(("parallel",)),
        name="proj_ffn" if has_proj else "ffn",
    )(*args)


def _head_norm(y, gain, grp_ref):
    pieces = []
    for c in range(D_MODEL // NORM_COLS):
        ys = y[:, c * NORM_COLS:(c + 1) * NORM_COLS]
        ms = _dot((ys * ys).astype(BF16), grp_ref[...])
        pieces.append(ys * lax.rsqrt(ms + EPS))
    return jnp.concatenate(pieces, axis=1) * gain


def _qkv_body(h_ref, gq_ref, gkv_ref, wq_ref, wk_ref, wv_ref, gqh_ref, gkh_ref, grp_ref, *outs, single_token):
    h = h_ref[...] if single_token else h_ref[0]
    n = _rms(h)
    hq = (n * gq_ref[...]).astype(BF16)
    c = (n * gkv_ref[...]).astype(BF16)
    q = _head_norm(_dot(hq, wq_ref[...]), gqh_ref[...], grp_ref)
    k = _head_norm(_dot(c, wk_ref[...]), gkh_ref[...], grp_ref)
    v = _dot(c, wv_ref[...])
    if single_token:
        q_ref, k_ref, v_ref = outs
        q_ref[...] = q
        k_ref[...] = k
        v_ref[...] = v
    else:
        qb_ref, kb_ref, vb_ref, k_ref, v_ref, mean_ref = outs
        qb_ref[0] = (q * SCALE).astype(BF16)
        kb_ref[0] = k.astype(BF16)
        vb_ref[0] = v.astype(BF16)
        for hh in range(N_HEADS):
            cols = slice(hh * HEAD_DIM, (hh + 1) * HEAD_DIM)
            k_ref[0, hh] = k[:, cols]
            v_ref[0, hh] = v[:, cols]
        mean_ref[0, 0] = jnp.mean(k, axis=0, keepdims=True)


def _group_mean_matrix():
    head = np.arange(NORM_COLS) // HEAD_DIM
    return jnp.asarray((head[:, None] == head[None, :]).astype(np.float32) / HEAD_DIM, dtype=BF16)


def _qkv_args(g_q_in, g_kv, w_q, w_k, w_v, g_q, g_k):
    args = [g_q_in[None, :], g_kv[None, :], w_q, w_k, w_v,
            jnp.tile(g_q, N_HEADS)[None, :], jnp.tile(g_k, N_HEADS)[None, :], _group_mean_matrix()]
    specs = [_const_spec((1, D_MODEL)), _const_spec((1, D_MODEL))] + [_const_spec((D_MODEL, D_MODEL))] * 3 + [
        _const_spec((1, D_MODEL)), _const_spec((1, D_MODEL)), _const_spec((NORM_COLS, NORM_COLS))]
    return args, specs


def _qkv_prompt(h, *weights):
    b, s, _ = h.shape
    nb = s // BLOCK
    args, specs = _qkv_args(*weights)
    tok_spec = pl.BlockSpec((1, BLOCK, D_MODEL), lambda bi, i: (bi, i, 0))
    head_spec = pl.BlockSpec((1, N_HEADS, BLOCK, HEAD_DIM), lambda bi, i: (bi, 0, i, 0))
    tok_shape = jax.ShapeDtypeStruct((b, s, D_MODEL), BF16)
    head_shape = jax.ShapeDtypeStruct((b, N_HEADS, s, HEAD_DIM), F32)
    return pl.pallas_call(
        functools.partial(_qkv_body, single_token=False),
        grid=(b, nb),
        in_specs=[tok_spec] + specs,
        out_specs=(tok_spec, tok_spec, tok_spec, head_spec, head_spec,
                   pl.BlockSpec((1, 1, 1, D_MODEL), lambda bi, i: (bi, i, 0, 0))),
        out_shape=(tok_shape, tok_shape, tok_shape, head_shape, head_shape,
                   jax.ShapeDtypeStruct((b, nb, 1, D_MODEL), F32)),
        compiler_params=_params(("parallel", "parallel")),
        name="qkv",
    )(h, *args)


def _qkv_decode(h, *weights):
    m = h.shape[0]
    args, specs = _qkv_args(*weights)
    row_spec = pl.BlockSpec((m, D_MODEL), lambda i: (0, 0))
    shape = jax.ShapeDtypeStruct((m, D_MODEL), F32)
    return pl.pallas_call(
        functools.partial(_qkv_body, single_token=True),
        grid=(1,),
        in_specs=[row_spec] + specs,
        out_specs=(row_spec, row_spec, row_spec),
        out_shape=(shape, shape, shape),
        compiler_params=_params(("arbitrary",)),
        name="qkv_decode",
    )(h, *args)


def _bucket_of_distance(n):
    n = np.asarray(n, dtype=np.int64)
    max_exact = N_BUCKETS // 2
    nf = np.maximum(n, 1).astype(np.float32)
    large = max_exact + (np.log(nf / np.float32(max_exact)) / np.float32(math.log(MAX_DISTANCE / max_exact))
                         * np.float32(N_BUCKETS - max_exact)).astype(np.int32)
    large = np.minimum(large, N_BUCKETS - 1)
    return np.where(n < max_exact, n, large).astype(np.int32)


def _bias_body(tab_ref, bm_ref, bv_ref, mat_ref, vec_ref):
    h = pl.program_id(0)
    mat = jnp.full(bm_ref.shape, NEG, F32)
    vec = jnp.full(bv_ref.shape, NEG, F32)
    for bkt in range(N_BUCKETS):
        val = tab_ref[bkt, h]
        mat = jnp.where(bm_ref[...] == bkt, val, mat)
        vec = jnp.where(bv_ref[...] == bkt, val, vec)
    mat_ref[0] = mat
    vec_ref[0] = vec


def _bias_tables(rel_table):
    key = np.arange(BLOCK)[:, None]
    qry = np.arange(BLOCK)[None, :]
    own = np.where(qry >= key, _bucket_of_distance(qry - key), -1)
    adj = _bucket_of_distance(BLOCK + qry - key)
    far = np.full((BLOCK, BLOCK), _bucket_of_distance(2 * BLOCK), np.int32)
    bm = jnp.asarray(np.stack([far, adj, own]).astype(np.int32))
    bv = jnp.asarray(_bucket_of_distance(BLOCK - np.arange(BLOCK))[:, None].astype(np.int32))
    return pl.pallas_call(
        _bias_body,
        grid=(N_HEADS,),
        in_specs=[pl.BlockSpec(memory_space=pltpu.SMEM), _const_spec((3, BLOCK, BLOCK)), _const_spec((BLOCK, 1))],
        out_specs=(pl.BlockSpec((1, 3, BLOCK, BLOCK), lambda h: (h, 0, 0, 0)),
                   pl.BlockSpec((1, BLOCK, 1), lambda h: (h, 0, 0))),
        out_shape=(jax.ShapeDtypeStruct((N_HEADS, 3, BLOCK, BLOCK), F32),
                   jax.ShapeDtypeStruct((N_HEADS, BLOCK, 1), F32)),
        compiler_params=_params(("arbitrary",)),
        name="rel_bias",
    )(rel_table, bm, bv)


def _top_blocks_penalty(gate, n_valid, own):
    nb = gate.shape[0]
    blk = lax.broadcasted_iota(jnp.int32, gate.shape, 0).astype(F32)
    valid = blk < n_valid
    g = jnp.where(valid, gate, NEG)
    sel = jnp.zeros(gate.shape, jnp.bool_)
    for _ in range(TOPK):
        mx = jnp.max(g, axis=0, keepdims=True)
        first = jnp.min(jnp.where(g == mx, blk, float(nb)), axis=0, keepdims=True)
        pick = blk == first
        sel = jnp.logical_or(sel, pick)
        g = jnp.where(pick, -jnp.inf, g)
    sel = jnp.logical_or(jnp.logical_and(sel, valid), blk == own)
    return jnp.where(sel, 0.0, NEG)


def _moba_body(q_ref, k_ref, v_ref, mean_ref, bias_ref, o_ref, vt_ref, qt_ref, pen_ref):
    i = pl.program_id(2)
    nb = vt_ref.shape[0]

    @pl.when(i == 0)
    def _():
        for jb in range(nb):
            vt_ref[jb] = v_ref[0, jb * BLOCK:(jb + 1) * BLOCK, :].astype(F32).T.astype(BF16)

    qt = q_ref[0].astype(F32).T
    feat = lax.broadcasted_iota(jnp.int32, qt.shape, 0)
    mean = mean_ref[0]
    mean_hi = mean.astype(BF16)
    mean_lo = (mean - mean_hi.astype(F32)).astype(BF16)
    i_f = i.astype(F32)
    for t in range(HEADS_PER_LANE_TILE):
        in_head = jnp.logical_and(feat >= t * HEAD_DIM, feat < (t + 1) * HEAD_DIM)
        qt_t = jnp.where(in_head, qt, 0.0).astype(BF16)
        qt_ref[t] = qt_t
        gate = _dot(mean_hi, qt_t) + _dot(mean_lo, qt_t)
        pen_ref[t] = _top_blocks_penalty(gate, i_f, i_f)

    def step(st, carry):
        j = i - st
        kind = jnp.maximum(2 - st, 0)
        kj = k_ref[0, pl.ds(pl.multiple_of(j * BLOCK, BLOCK), BLOCK), :]
        vtj = vt_ref[j]
        new = []
        for t in range(HEADS_PER_LANE_TILE):
            m, l, acc = carry[t]
            s = _dot(kj, qt_ref[t]) + bias_ref[t, kind] + pen_ref[t, pl.ds(j, 1), :]
            m_new = jnp.maximum(m, jnp.max(s, axis=0, keepdims=True))
            alpha = jnp.exp(m - m_new)
            p = jnp.exp(s - m_new)
            l = alpha * l + jnp.sum(p, axis=0, keepdims=True)
            acc = alpha * acc + _dot(vtj[t * HEAD_DIM:(t + 1) * HEAD_DIM, :], p.astype(BF16))
            new.append((m_new, l, acc))
        return tuple(new)

    init = tuple((jnp.full((1, BLOCK), -jnp.inf, F32), jnp.zeros((1, BLOCK), F32),
                  jnp.zeros((HEAD_DIM, BLOCK), F32)) for _ in range(HEADS_PER_LANE_TILE))
    res = lax.fori_loop(0, i + 1, step, init)
    ot = jnp.concatenate([acc / l for (_, l, acc) in res], axis=0)
    o_ref[0] = ot.T.astype(o_ref.dtype)


def _moba_prompt(qb, kb, vb, means, bias_mat):
    b, s, _ = qb.shape
    nb = s // BLOCK
    hp = D_MODEL // V7X_LANES
    return pl.pallas_call(
        _moba_body,
        grid=(b, hp, nb),
        in_specs=[pl.BlockSpec((1, BLOCK, V7X_LANES), lambda bi, p, i: (bi, i, p)),
                  pl.BlockSpec((1, s, V7X_LANES), lambda bi, p, i: (bi, 0, p)),
                  pl.BlockSpec((1, s, V7X_LANES), lambda bi, p, i: (bi, 0, p)),
                  pl.BlockSpec((1, nb, V7X_LANES), lambda bi, p, i: (bi, 0, p)),
                  pl.BlockSpec((HEADS_PER_LANE_TILE, 3, BLOCK, BLOCK), lambda bi, p, i: (p, 0, 0, 0))],
        out_specs=pl.BlockSpec((1, BLOCK, V7X_LANES), lambda bi, p, i: (bi, i, p)),
        out_shape=jax.ShapeDtypeStruct((b, s, D_MODEL), BF16),
        scratch_shapes=[pltpu.VMEM((nb, V7X_LANES, BLOCK), BF16),
                        pltpu.VMEM((HEADS_PER_LANE_TILE, V7X_LANES, BLOCK), BF16),
                        pltpu.VMEM((HEADS_PER_LANE_TILE, nb, BLOCK), F32)],
        compiler_params=_params(("parallel", "parallel", "arbitrary")),
        name="moba_prompt",
    )(qb, kb, vb, means, bias_mat)


def _page_mean_body(pt_ref, *refs):
    del pt_ref
    pages, o_ref = refs[:-1], refs[-1]
    for t in range(len(pages) // PAGES_PER_BLOCK):
        tot = jnp.sum(pages[PAGES_PER_BLOCK * t][0], axis=1)
        for p in range(1, PAGES_PER_BLOCK):
            tot = tot + jnp.sum(pages[PAGES_PER_BLOCK * t + p][0], axis=1)
        o_ref[0, t] = tot * (1.0 / BLOCK)


def _cached_block_means(cache_k, page_table):
    b, n_pages = page_table.shape
    n_blocks = n_pages // PAGES_PER_BLOCK
    used_pages = n_blocks * PAGES_PER_BLOCK
    steps = used_pages // MEAN_PAGES_PER_STEP
    blocks_per_step = MEAN_PAGES_PER_STEP // PAGES_PER_BLOCK

    def page_spec(t):
        return pl.BlockSpec((1, N_HEADS, PAGE_SIZE, HEAD_DIM),
                            lambda bi, g, pt: (pt[bi, g * MEAN_PAGES_PER_STEP + t], 0, 0, 0))

    return pl.pallas_call(
        _page_mean_body,
        grid_spec=pltpu.PrefetchScalarGridSpec(
            num_scalar_prefetch=1,
            grid=(b, steps),
            in_specs=[page_spec(t) for t in range(MEAN_PAGES_PER_STEP)],
            out_specs=pl.BlockSpec((1, blocks_per_step, N_HEADS, HEAD_DIM), lambda bi, g, pt: (bi, g, 0, 0)),
        ),
        out_shape=jax.ShapeDtypeStruct((b, n_blocks, N_HEADS, HEAD_DIM), F32),
        compiler_params=_params(("parallel", "parallel")),
        name="page_means",
    )(page_table, *([cache_k] * MEAN_PAGES_PER_STEP))


def _decode_topk_body(q_ref, cm_ref, idx_ref):
    n_blocks = cm_ref.shape[2]
    lane = lax.broadcasted_iota(jnp.int32, (1, N_HEADS), 1)
    blk = lax.broadcasted_iota(jnp.int32, (n_blocks, 1), 0).astype(F32)
    rows = [jnp.zeros((1, N_HEADS), F32) for _ in range(TOPK)]
    for h in range(N_HEADS):
        g = jnp.sum(cm_ref[0, h] * q_ref[0, h], axis=-1, keepdims=True)
        for r in range(TOPK):
            mx = jnp.max(g, axis=0, keepdims=True)
            first = jnp.min(jnp.where(g == mx, blk, float(n_blocks)), axis=0, keepdims=True)
            rows[r] = jnp.where(lane == h, first, rows[r])
            g = jnp.where(blk == first, -jnp.inf, g)
    idx_ref[0] = jnp.concatenate(rows, axis=0).astype(jnp.int32)


def _decode_topk(q, cmeans):
    b = q.shape[0]
    n_blocks = cmeans.shape[2]
    return pl.pallas_call(
        _decode_topk_body,
        grid=(b,),
        in_specs=[pl.BlockSpec((1, N_HEADS, 1, HEAD_DIM), lambda bi: (bi, 0, 0, 0)),
                  pl.BlockSpec((1, N_HEADS, n_blocks, HEAD_DIM), lambda bi: (bi, 0, 0, 0))],
        out_specs=pl.BlockSpec((1, TOPK, N_HEADS), lambda bi: (bi, 0, 0)),
        out_shape=jax.ShapeDtypeStruct((b, TOPK, N_HEADS), jnp.int32),
        compiler_params=_params(("parallel",)),
        name="decode_topk",
    )(q, cmeans)


def _decode_attn_body(pt_ref, idx_ref, tab_ref, q_ref, kn_ref, vn_ref, near_ref, *refs, n_blocks):
    del pt_ref
    n_sel = TOPK * PAGES_PER_BLOCK
    k_pages, v_pages, o_ref = refs[:n_sel], refs[n_sel:2 * n_sel], refs[-1]
    bi = pl.program_id(0)
    h = pl.program_id(1)
    q = q_ref[0, 0]
    far = tab_ref[N_BUCKETS - 1, h]
    s_own = jnp.sum(q * kn_ref[0, 0], axis=-1, keepdims=True) * SCALE + tab_ref[0, h]
    scores = []
    m = s_own
    for r in range(TOPK):
        blk = idx_ref[bi, r * N_HEADS + h]
        ok = blk < n_blocks
        adjacent = blk == n_blocks - 1
        for p in range(PAGES_PER_BLOCK):
            kp = k_pages[r * PAGES_PER_BLOCK + p][0, 0]
            s = jnp.sum(kp * q, axis=-1, keepdims=True) * SCALE
            bias = jnp.where(adjacent, near_ref[0, p * PAGE_SIZE:(p + 1) * PAGE_SIZE, :], far)
            s = jnp.where(ok, s + bias, NEG)
            scores.append(s)
            m = jnp.maximum(m, jnp.max(s, axis=0, keepdims=True))
    p_own = jnp.exp(s_own - m)
    l = p_own
    acc = p_own * vn_ref[0, 0]
    for s, vp in zip(scores, v_pages):
        p = jnp.exp(s - m)
        l = l + jnp.sum(p, axis=0, keepdims=True)
        acc = acc + jnp.sum(p * vp[0, 0], axis=0, keepdims=True)
    o_ref[0, 0] = acc / l


def _decode_attn(q, k_new, v_new, cache_k, cache_v, page_table, idx, rel_table, near_bias):
    b, n_pages = page_table.shape
    n_blocks = n_pages // PAGES_PER_BLOCK
    tok_spec = pl.BlockSpec((1, 1, 1, HEAD_DIM), lambda bi, h, pt, ix: (bi, h, 0, 0))

    def page_spec(r, p):
        def index(bi, h, pt, ix):
            logical = jnp.minimum(ix[bi, r * N_HEADS + h] * PAGES_PER_BLOCK + p, n_pages - 1)
            return (pt[bi, logical], h, 0, 0)
        return pl.BlockSpec((1, 1, PAGE_SIZE, HEAD_DIM), index)

    page_specs = [page_spec(r, p) for r in range(TOPK) for p in range(PAGES_PER_BLOCK)]
    n_sel = len(page_specs)
    return pl.pallas_call(
        functools.partial(_decode_attn_body, n_blocks=n_blocks),
        grid_spec=pltpu.PrefetchScalarGridSpec(
            num_scalar_prefetch=2,
            grid=(b, N_HEADS),
            in_specs=[pl.BlockSpec(memory_space=pltpu.SMEM), tok_spec, tok_spec, tok_spec,
                      pl.BlockSpec((1, BLOCK, 1), lambda bi, h, pt, ix: (h, 0, 0))] + page_specs + page_specs,
            out_specs=tok_spec,
        ),
        out_shape=jax.ShapeDtypeStruct(q.shape, F32),
        compiler_params=_params(("parallel", "parallel")),
        name="decode_attn",
    )(page_table, idx, rel_table, q, k_new, v_new, near_bias, *([cache_k] * n_sel), *([cache_v] * n_sel))


def _pick_tile(m, target):
    return target if m % target == 0 else m


def kernel(x_prompt, x_sample, cache_k, cache_v, page_table, rel_table, g_mix, g_ffn, w_gin, g_gv, w_sp, b_sp,
           w_gout, g_kv, w_k, w_v, g_k, w_q, g_q, w_o, w_f1, w_f3, w_f2):
    bsz, seq, _ = x_prompt.shape
    dec = x_sample.shape[0]
    cast = lambda w: w.astype(BF16)
    w_gin_b, w_gout_b = cast(w_gin[0]), cast(w_gout[0])
    w_q_b, w_k_b, w_v_b, w_o_b = cast(w_q[0]), cast(w_k), cast(w_v), cast(w_o[0])
    w_f1_b, w_f3_b, w_f2_b = cast(w_f1), cast(w_f3), cast(w_f2)
    qkv_w = (g_mix[1], g_kv, w_q_b, w_k_b, w_v_b, g_q[0], g_k)

    bias_mat, bias_near = _bias_tables(rel_table)

    xp = x_prompt.reshape(bsz * seq, D_MODEL)
    tm = _pick_tile(bsz * seq, 512)
    h = _gmlp(xp, g_mix[0], w_gin_b, g_gv[0], w_sp[0], b_sp[0], w_gout_b, single_token=False, tm=tm)
    h = _ffn(h, g_ffn[0], w_f1_b[0], w_f3_b[0], w_f2_b[0], tm=tm)
    qb, kb, vb, prompt_k, prompt_v, means = _qkv_prompt(h.reshape(bsz, seq, D_MODEL), *qkv_w)
    attn = _moba_prompt(qb, kb, vb, means.reshape(bsz, seq // BLOCK, D_MODEL), bias_mat)
    y_prompt = _ffn(h, g_ffn[1], w_f1_b[1], w_f3_b[1], w_f2_b[1], tm=tm,
                    attn=attn.reshape(bsz * seq, D_MODEL), w_o=w_o_b).reshape(bsz, seq, D_MODEL)

    xs = x_sample.reshape(dec, D_MODEL)
    hs, v_rows = _gmlp(xs, g_mix[0], w_gin_b, g_gv[0], w_sp[0], b_sp[0], w_gout_b, single_token=True, tm=dec)
    hs = _ffn(hs, g_ffn[0], w_f1_b[0], w_f3_b[0], w_f2_b[0], tm=dec)
    qs, ks, vs = (a.reshape(dec, N_HEADS, 1, HEAD_DIM) for a in _qkv_decode(hs, *qkv_w))
    cmeans = _cached_block_means(cache_k, page_table).transpose(0, 2, 1, 3)
    idx = _decode_topk(qs, cmeans).reshape(dec, TOPK * N_HEADS)
    attn_s = _decode_attn(qs, ks, vs, cache_k, cache_v, page_table, idx, rel_table, bias_near)
    y_sample = _ffn(hs, g_ffn[1], w_f1_b[1], w_f3_b[1], w_f2_b[1], tm=dec,
                    attn=attn_s.reshape(dec, D_MODEL).astype(BF16), w_o=w_o_b).reshape(dec, 1, D_MODEL)

    return (y_prompt, y_sample, prompt_k, prompt_v, ks, vs, v_rows.reshape(1, dec, 1, D_GATE))
```

```python
import functools
import math

import numpy as np
import jax
import jax.numpy as jnp
from jax import lax
from jax.experimental import pallas as pl
from jax.experimental.pallas import tpu as pltpu

F32 = jnp.float32
BF16 = jnp.bfloat16

D_MODEL = 1024
N_HEADS = 16
HEAD_DIM = 64
CHUNK = 128
N_GROUPS = 8
GROUP_DIM = 128
D_GATE = 1024
BLOCK = 256
TOPK = 3
N_BUCKETS = 32
MAX_DISTANCE = 128
PAGE_SIZE = 128
PAGES_PER_BLOCK = BLOCK // PAGE_SIZE
EPS = 1e-6
NEG = -1e30
SCALE = HEAD_DIM ** -0.5

V7X_LANES = 128
V7X_MXU_DIM = 256
V7X_VMEM_LIMIT = 56 * 1024 * 1024

LOG2E = math.log2(math.e)
FFN_CHUNK = V7X_MXU_DIM
HEADS_PER_LANE_TILE = V7X_LANES // HEAD_DIM
NORM_COLS = V7X_MXU_DIM
MEAN_BLOCKS_PER_STEP = 8
KEY_BLOCKS_PER_STEP = 4
KEY_ROWS_PER_STEP = KEY_BLOCKS_PER_STEP * BLOCK


def _rms(x):
    return x * lax.rsqrt(jnp.mean(x * x, axis=-1, keepdims=True) + EPS)


def _gelu_tanh(x):
    c = math.sqrt(2.0 / math.pi)
    return x * (0.5 * (1.0 + jnp.tanh(c * (x + 0.044715 * (x * x * x)))))


def _dot(a, b):
    return jnp.dot(a, b, preferred_element_type=F32)


def _params(sem, vmem=V7X_VMEM_LIMIT):
    return pltpu.CompilerParams(dimension_semantics=sem, vmem_limit_bytes=vmem)


def _const_spec(shape):
    zeros = (0,) * len(shape)
    return pl.BlockSpec(shape, lambda *_: zeros, pipeline_mode=pl.Buffered(1))


def _gmlp_body(x_ref, gmix_ref, win_ref, ggv_ref, wsp_ref, bsp_ref, wout_ref, *rest, single_token):
    x = x_ref[...]
    hn = (_rms(x) * gmix_ref[...]).astype(BF16)
    uv = _gelu_tanh(_dot(hn, win_ref[...]))
    u = uv[:, :D_GATE]
    v = _rms(uv[:, D_GATE:]) * ggv_ref[...]
    if single_token:
        h_ref, v_ref = rest
        v_ref[...] = v
        mixed = v * wsp_ref[...] + bsp_ref[...]
    else:
        h_ref, mixed_ref = rest
        vb = v.astype(BF16)
        row = lax.broadcasted_iota(jnp.int32, (CHUNK, CHUNK), 0)
        col = lax.broadcasted_iota(jnp.int32, (CHUNK, CHUNK), 1)
        for g in range(N_GROUPS):
            cols = slice(g * GROUP_DIM, (g + 1) * GROUP_DIM)
            w = jnp.where(row >= col, wsp_ref[g], 0.0).astype(BF16)
            for n in range(x.shape[0] // CHUNK):
                rows = slice(n * CHUNK, (n + 1) * CHUNK)
                mixed_ref[rows, cols] = _dot(w, vb[rows, cols]) + bsp_ref[:, cols]
        mixed = mixed_ref[...]
    h_ref[...] = x + _dot((u * mixed).astype(BF16), wout_ref[...])


def _gmlp(x, g_mix, w_in, g_gv, w_sp, b_sp, w_out, *, single_token, tm):
    m = x.shape[0]
    row_spec = pl.BlockSpec((tm, D_MODEL), lambda i: (i, 0))
    if single_token:
        wsp_arg = jnp.repeat(w_sp[:, 0, 0], GROUP_DIM)[None, :]
        bsp_arg = jnp.repeat(b_sp[:, 0], GROUP_DIM)[None, :]
        wsp_spec = _const_spec((1, D_GATE))
        bsp_spec = _const_spec((1, D_GATE))
        out_shape = (jax.ShapeDtypeStruct((m, D_MODEL), F32), jax.ShapeDtypeStruct((m, D_GATE), F32))
        out_specs = (row_spec, pl.BlockSpec((tm, D_GATE), lambda i: (i, 0)))
        scratch = []
    else:
        wsp_arg = w_sp
        bsp_arg = jnp.repeat(b_sp.T, GROUP_DIM, axis=1)
        wsp_spec = _const_spec((N_GROUPS, CHUNK, CHUNK))
        bsp_spec = _const_spec((CHUNK, D_GATE))
        out_shape = jax.ShapeDtypeStruct((m, D_MODEL), F32)
        out_specs = row_spec
        scratch = [pltpu.VMEM((tm, D_GATE), F32)]
    return pl.pallas_call(
        functools.partial(_gmlp_body, single_token=single_token),
        grid=(m // tm,),
        in_specs=[row_spec, _const_spec((1, D_MODEL)), _const_spec((D_MODEL, 2 * D_GATE)),
                  _const_spec((1, D_GATE)), wsp_spec, bsp_spec, _const_spec((D_GATE, D_MODEL))],
        out_specs=out_specs,
        out_shape=out_shape,
        scratch_shapes=scratch,
        compiler_params=_params(("parallel",)),
        name="gmlp_decode" if single_token else "gmlp",
    )(x, g_mix[None, :], w_in, g_gv[None, :], wsp_arg, bsp_arg, w_out)


def _ffn_body(*refs, has_proj):
    if has_proj:
        h_ref, a_ref, wo_ref, g_ref, w1_ref, w3_ref, w2_ref, o_ref = refs
        h = h_ref[...] + _dot(a_ref[...], wo_ref[...])
    else:
        h_ref, g_ref, w1_ref, w3_ref, w2_ref, o_ref = refs
        h = h_ref[...]
    n = (_rms(h) * g_ref[...]).astype(BF16)
    acc = h
    for c in range(w1_ref.shape[1] // FFN_CHUNK):
        cols = slice(c * FFN_CHUNK, (c + 1) * FFN_CHUNK)
        a = _dot(n, w1_ref[:, cols])
        b = _dot(n, w3_ref[:, cols])
        acc = acc + _dot((a * jax.nn.sigmoid(a) * b).astype(BF16), w2_ref[cols, :])
    o_ref[...] = acc


def _ffn(h, g, w1, w3, w2, *, tm, attn=None, w_o=None):
    m = h.shape[0]
    d_ff = w1.shape[1]
    row_spec = pl.BlockSpec((tm, D_MODEL), lambda i: (i, 0))
    has_proj = attn is not None
    args = [h]
    specs = [row_spec]
    if has_proj:
        args += [attn, w_o]
        specs += [row_spec, _const_spec((D_MODEL, D_MODEL))]
    args += [g[None, :], w1, w3, w2]
    specs += [_const_spec((1, D_MODEL)), _const_spec((D_MODEL, d_ff)), _const_spec((D_MODEL, d_ff)),
              _const_spec((d_ff, D_MODEL))]
    return pl.pallas_call(
        functools.partial(_ffn_body, has_proj=has_proj),
        grid=(m // tm,),
        in_specs=specs,
        out_specs=row_spec,
        out_shape=jax.ShapeDtypeStruct((m, D_MODEL), F32),
        compiler_params=_params(("parallel",)),
        name="proj_ffn" if has_proj else "ffn",
    )(*args)


def _qkv_prompt_body(h_ref, gq_ref, gkv_ref, wq_ref, wk_ref, wv_ref, gqh_ref, gkh_ref,
                     qt_ref, vt_ref, kb_ref, k_ref, v_ref, mean_ref):
    n_t = _rms(h_ref[0]).T
    tok = n_t.shape[1]
    hq = (n_t * gq_ref[...]).astype(BF16)
    c = (n_t * gkv_ref[...]).astype(BF16)

    def head_norm(y, g_ref):
        y3 = y.reshape(N_HEADS, HEAD_DIM, tok)
        ms = jnp.mean(y3 * y3, axis=1, keepdims=True)
        return y3 * lax.rsqrt(ms + EPS) * g_ref[...]

    q3 = head_norm(_dot(wq_ref[...], hq), gqh_ref)
    k3 = head_norm(_dot(wk_ref[...], c), gkh_ref)
    v = _dot(wv_ref[...], c)
    qt_ref[0] = (q3 * (SCALE * LOG2E)).reshape(D_MODEL, tok).astype(BF16)
    vt_ref[0, 0] = v.astype(BF16)
    k_ref[0] = k3
    v_ref[0] = v.reshape(N_HEADS, HEAD_DIM, tok)
    k_tok = k3.reshape(D_MODEL, tok).T
    kb_ref[0] = k_tok.astype(BF16)
    mean_ref[0, 0] = jnp.mean(k_tok, axis=0, keepdims=True)


def _qkv_prompt(h, g_q_in, g_kv, w_q, w_k, w_v, g_q, g_k):
    b, s, _ = h.shape
    nb = s // BLOCK
    col = lambda g: g[:, None]
    args = [col(g_q_in), col(g_kv), w_q.T, w_k.T, w_v.T, col(g_q), col(g_k)]
    specs = ([_const_spec((D_MODEL, 1))] * 2 + [_const_spec((D_MODEL, D_MODEL))] * 3
             + [_const_spec((HEAD_DIM, 1))] * 2)
    head_spec = pl.BlockSpec((1, N_HEADS, HEAD_DIM, BLOCK), lambda bi, i: (bi, 0, 0, i))
    head_shape = jax.ShapeDtypeStruct((b, N_HEADS, HEAD_DIM, s), F32)
    return pl.pallas_call(
        _qkv_prompt_body,
        grid=(b, nb),
        in_specs=[pl.BlockSpec((1, BLOCK, D_MODEL), lambda bi, i: (bi, i, 0))] + specs,
        out_specs=(pl.BlockSpec((1, D_MODEL, BLOCK), lambda bi, i: (bi, 0, i)),
                   pl.BlockSpec((1, 1, D_MODEL, BLOCK), lambda bi, i: (bi, i, 0, 0)),
                   pl.BlockSpec((1, BLOCK, D_MODEL), lambda bi, i: (bi, i, 0)),
                   head_spec, head_spec,
                   pl.BlockSpec((1, 1, 1, D_MODEL), lambda bi, i: (bi, i, 0, 0))),
        out_shape=(jax.ShapeDtypeStruct((b, D_MODEL, s), BF16),
                   jax.ShapeDtypeStruct((b, nb, D_MODEL, BLOCK), BF16),
                   jax.ShapeDtypeStruct((b, s, D_MODEL), BF16),
                   head_shape, head_shape,
                   jax.ShapeDtypeStruct((b, nb, 1, D_MODEL), F32)),
        compiler_params=_params(("parallel", "parallel")),
        name="qkv",
    )(h, *args)


def _qkv_decode_body(h_ref, gq_ref, gkv_ref, wq_ref, wk_ref, wv_ref, gqh_ref, gkh_ref, grp_ref,
                     q_ref, k_ref, v_ref):
    n = _rms(h_ref[...])
    hq = (n * gq_ref[...]).astype(BF16)
    c = (n * gkv_ref[...]).astype(BF16)

    def head_norm(y, g_ref):
        pieces = []
        for cb in range(D_MODEL // NORM_COLS):
            ys = y[:, cb * NORM_COLS:(cb + 1) * NORM_COLS]
            ms = _dot((ys * ys).astype(BF16), grp_ref[...])
            pieces.append(ys * lax.rsqrt(ms + EPS))
        return jnp.concatenate(pieces, axis=1) * g_ref[...]

    q_ref[...] = head_norm(_dot(hq, wq_ref[...]), gqh_ref)
    k_ref[...] = head_norm(_dot(c, wk_ref[...]), gkh_ref)
    v_ref[...] = _dot(c, wv_ref[...])


def _qkv_decode(h, g_q_in, g_kv, w_q, w_k, w_v, g_q, g_k):
    m = h.shape[0]
    head = np.arange(NORM_COLS) // HEAD_DIM
    grp = jnp.asarray((head[:, None] == head[None, :]).astype(np.float32) / HEAD_DIM, dtype=BF16)
    args = [g_q_in[None, :], g_kv[None, :], w_q, w_k, w_v,
            jnp.tile(g_q, N_HEADS)[None, :], jnp.tile(g_k, N_HEADS)[None, :], grp]
    specs = ([_const_spec((1, D_MODEL))] * 2 + [_const_spec((D_MODEL, D_MODEL))] * 3
             + [_const_spec((1, D_MODEL))] * 2 + [_const_spec((NORM_COLS, NORM_COLS))])
    row_spec = pl.BlockSpec((m, D_MODEL), lambda i: (0, 0))
    shape = jax.ShapeDtypeStruct((m, D_MODEL), F32)
    return pl.pallas_call(
        _qkv_decode_body,
        grid=(1,),
        in_specs=[row_spec] + specs,
        out_specs=(row_spec, row_spec, row_spec),
        out_shape=(shape, shape, shape),
        compiler_params=_params(("arbitrary",)),
        name="qkv_decode",
    )(h, *args)


def _bucket_of_distance(n):
    n = np.asarray(n, dtype=np.int64)
    max_exact = N_BUCKETS // 2
    nf = np.maximum(n, 1).astype(np.float32)
    large = max_exact + (np.log(nf / np.float32(max_exact)) / np.float32(math.log(MAX_DISTANCE / max_exact))
                         * np.float32(N_BUCKETS - max_exact)).astype(np.int32)
    large = np.minimum(large, N_BUCKETS - 1)
    return np.where(n < max_exact, n, large).astype(np.int32)


def _bias_body(tab_ref, bm_ref, bv_ref, mat_ref, vec_ref):
    h = pl.program_id(0)
    mat = jnp.full(bm_ref.shape, NEG, F32)
    vec = jnp.full(bv_ref.shape, NEG, F32)
    for bkt in range(N_BUCKETS):
        val = tab_ref[bkt, h]
        mat = jnp.where(bm_ref[...] == bkt, val * LOG2E, mat)
        vec = jnp.where(bv_ref[...] == bkt, val, vec)
    mat_ref[0] = mat
    vec_ref[0] = vec


def _bias_tables(rel_table):
    key = np.arange(BLOCK)[:, None]
    qry = np.arange(BLOCK)[None, :]
    own = np.where(qry >= key, _bucket_of_distance(qry - key), -1)
    adj = _bucket_of_distance(BLOCK + qry - key)
    far = np.full((BLOCK, BLOCK), _bucket_of_distance(2 * BLOCK), np.int32)
    bm = jnp.asarray(np.stack([far, adj, own]).astype(np.int32))
    bv = jnp.asarray(_bucket_of_distance(BLOCK - np.arange(BLOCK))[None, :].astype(np.int32))
    return pl.pallas_call(
        _bias_body,
        grid=(N_HEADS,),
        in_specs=[pl.BlockSpec(memory_space=pltpu.SMEM), _const_spec((3, BLOCK, BLOCK)), _const_spec((1, BLOCK))],
        out_specs=(pl.BlockSpec((1, 3, BLOCK, BLOCK), lambda h: (h, 0, 0, 0)),
                   pl.BlockSpec((1, 1, BLOCK), lambda h: (h, 0, 0))),
        out_shape=(jax.ShapeDtypeStruct((N_HEADS, 3, BLOCK, BLOCK), F32),
                   jax.ShapeDtypeStruct((N_HEADS, 1, BLOCK), F32)),
        compiler_params=_params(("arbitrary",)),
        name="rel_bias",
    )(rel_table, bm, bv)


def _top_blocks_penalty(gate, n_valid, own):
    nb = gate.shape[0]
    blk = lax.broadcasted_iota(jnp.int32, gate.shape, 0).astype(F32)
    valid = blk < n_valid
    g = jnp.where(valid, gate, NEG)
    sel = jnp.zeros(gate.shape, jnp.bool_)
    for _ in range(TOPK):
        mx = jnp.max(g, axis=0, keepdims=True)
        first = jnp.min(jnp.where(g == mx, blk, float(nb)), axis=0, keepdims=True)
        pick = blk == first
        sel = jnp.logical_or(sel, pick)
        g = jnp.where(pick, -jnp.inf, g)
    sel = jnp.logical_or(jnp.logical_and(sel, valid), blk == own)
    return jnp.where(sel, 0.0, NEG)


def _moba_body(qt_ref, k_ref, vt_ref, mean_ref, bias_ref, o_ref, qm_ref, pen_ref):
    i = pl.program_id(2)
    qt = qt_ref[0]
    feat = lax.broadcasted_iota(jnp.int32, qt.shape, 0)
    mean = mean_ref[0]
    mean_hi = mean.astype(BF16)
    mean_lo = (mean - mean_hi.astype(F32)).astype(BF16)
    i_f = i.astype(F32)
    for t in range(HEADS_PER_LANE_TILE):
        in_head = jnp.logical_and(feat >= t * HEAD_DIM, feat < (t + 1) * HEAD_DIM)
        qt_t = jnp.where(in_head, qt, jnp.zeros_like(qt))
        qm_ref[t] = qt_t
        gate = _dot(mean_hi, qt_t) + _dot(mean_lo, qt_t)
        pen_ref[t] = _top_blocks_penalty(gate, i_f, i_f)

    def rows(j):
        return pl.ds(pl.multiple_of(j * BLOCK, BLOCK), BLOCK)

    def head_rows(t):
        return slice(t * HEAD_DIM, (t + 1) * HEAD_DIM)

    j_adj = jnp.maximum(i - 1, 0)
    no_adj = jnp.where(i >= 1, 0.0, NEG)
    k_own = k_ref[0, rows(i), :]
    k_adj = k_ref[0, rows(j_adj), :]
    heads = range(HEADS_PER_LANE_TILE)
    s_own = [_dot(k_own, qm_ref[t]) for t in heads]
    s_adj = [_dot(k_adj, qm_ref[t]) for t in heads]
    s_own = [s_own[t] + bias_ref[t, 2] for t in heads]
    s_adj = [s_adj[t] + bias_ref[t, 1] for t in heads]
    r_adj = [pen_ref[t, pl.ds(j_adj, 1), :] + no_adj for t in heads]
    m0 = [jnp.maximum(jnp.max(s_own[t], axis=0, keepdims=True),
                      jnp.max(s_adj[t], axis=0, keepdims=True) + r_adj[t]) for t in heads]
    init = []
    for t in heads:
        p_own = jnp.exp2(s_own[t] - m0[t])
        p_adj = jnp.exp2(s_adj[t] - (m0[t] - r_adj[t]))
        l = jnp.sum(p_own, axis=0, keepdims=True) + jnp.sum(p_adj, axis=0, keepdims=True)
        acc = (_dot(vt_ref[0, i, head_rows(t), :], p_own.astype(BF16))
               + _dot(vt_ref[0, j_adj, head_rows(t), :], p_adj.astype(BF16)))
        init.append((m0[t], l, acc))

    n_far = jnp.maximum(i - 1, 0)

    def step(c, carry):
        kc = k_ref[0, pl.ds(pl.multiple_of(c * KEY_ROWS_PER_STEP, KEY_ROWS_PER_STEP), KEY_ROWS_PER_STEP), :]
        heads = range(HEADS_PER_LANE_TILE)
        scores = [_dot(kc, qm_ref[t]) for t in heads]
        offs, m_new = [], []
        for t in heads:
            far_bias = bias_ref[t, 0, 0:1, 0:1]
            mt = carry[t][0]
            rs = []
            for sub in range(KEY_BLOCKS_PER_STEP):
                j = c * KEY_BLOCKS_PER_STEP + sub
                r = pen_ref[t, pl.ds(j, 1), :] + far_bias + jnp.where(j < n_far, 0.0, NEG)
                rs.append(r)
                mt = jnp.maximum(mt, jnp.max(scores[t][sub * BLOCK:(sub + 1) * BLOCK], axis=0, keepdims=True) + r)
            offs.append(rs)
            m_new.append(mt)
        new = []
        for t in heads:
            m, l, acc = carry[t]
            alpha = jnp.exp2(m - m_new[t])
            l = alpha * l
            acc = alpha * acc
            for sub in range(KEY_BLOCKS_PER_STEP):
                p = jnp.exp2(scores[t][sub * BLOCK:(sub + 1) * BLOCK] - (m_new[t] - offs[t][sub]))
                l = l + jnp.sum(p, axis=0, keepdims=True)
                acc = acc + _dot(vt_ref[0, c * KEY_BLOCKS_PER_STEP + sub, head_rows(t), :], p.astype(BF16))
            new.append((m_new[t], l, acc))
        return tuple(new)

    n_steps = (n_far + KEY_BLOCKS_PER_STEP - 1) // KEY_BLOCKS_PER_STEP
    res = lax.fori_loop(0, n_steps, step, tuple(init))
    ot = jnp.concatenate([acc / l for (_, l, acc) in res], axis=0)
    o_ref[0] = ot.T.astype(o_ref.dtype)


def _moba_prompt(qt, kb, vt, means, bias_mat):
    b, s, _ = kb.shape
    nb = s // BLOCK
    assert pl.cdiv(max(nb - 2, 0), KEY_BLOCKS_PER_STEP) * KEY_BLOCKS_PER_STEP <= nb
    hp = D_MODEL // V7X_LANES
    return pl.pallas_call(
        _moba_body,
        grid=(b, hp, nb),
        in_specs=[pl.BlockSpec((1, V7X_LANES, BLOCK), lambda bi, p, i: (bi, p, i)),
                  pl.BlockSpec((1, s, V7X_LANES), lambda bi, p, i: (bi, 0, p)),
                  pl.BlockSpec((1, nb, V7X_LANES, BLOCK), lambda bi, p, i: (bi, 0, p, 0)),
                  pl.BlockSpec((1, nb, V7X_LANES), lambda bi, p, i: (bi, 0, p)),
                  pl.BlockSpec((HEADS_PER_LANE_TILE, 3, BLOCK, BLOCK), lambda bi, p, i: (p, 0, 0, 0))],
        out_specs=pl.BlockSpec((1, BLOCK, V7X_LANES), lambda bi, p, i: (bi, i, p)),
        out_shape=jax.ShapeDtypeStruct((b, s, D_MODEL), BF16),
        scratch_shapes=[pltpu.VMEM((HEADS_PER_LANE_TILE, V7X_LANES, BLOCK), BF16),
                        pltpu.VMEM((HEADS_PER_LANE_TILE, nb, BLOCK), F32)],
        compiler_params=_params(("parallel", "parallel", "arbitrary")),
        name="moba_prompt",
    )(qt, kb, vt, means, bias_mat)


def _split_bf16(x):
    hi = x.astype(BF16)
    return hi, (x - hi.astype(F32)).astype(BF16)


def _page_mean_body(pt_ref, *refs):
    del pt_ref
    pages, o_ref = refs[:-1], refs[-1]
    ones = jnp.ones((8, PAGE_SIZE), BF16)
    contract_minor = (((1,), (1,)), ((), ()))
    rows = []
    for t in range(len(pages) // PAGES_PER_BLOCK):
        tot = pages[PAGES_PER_BLOCK * t][0]
        for p in range(1, PAGES_PER_BLOCK):
            tot = tot + pages[PAGES_PER_BLOCK * t + p][0]
        hi, lo = _split_bf16(tot.reshape(D_MODEL, PAGE_SIZE))
        sums = (lax.dot_general(ones, hi, contract_minor, preferred_element_type=F32)
                + lax.dot_general(ones, lo, contract_minor, preferred_element_type=F32))
        rows.append(sums[0:1])
    o_ref[0] = jnp.concatenate(rows, axis=0) * (1.0 / BLOCK)


def _cached_block_means(cache_kt, page_table):
    b, n_pages = page_table.shape
    n_blocks = n_pages // PAGES_PER_BLOCK
    steps = n_blocks // MEAN_BLOCKS_PER_STEP
    pages_per_step = MEAN_BLOCKS_PER_STEP * PAGES_PER_BLOCK

    def page_spec(t):
        return pl.BlockSpec((1, N_HEADS, HEAD_DIM, PAGE_SIZE),
                            lambda bi, g, pt: (pt[bi, g * pages_per_step + t], 0, 0, 0))

    return pl.pallas_call(
        _page_mean_body,
        grid_spec=pltpu.PrefetchScalarGridSpec(
            num_scalar_prefetch=1,
            grid=(b, steps),
            in_specs=[page_spec(t) for t in range(pages_per_step)],
            out_specs=pl.BlockSpec((1, MEAN_BLOCKS_PER_STEP, D_MODEL), lambda bi, g, pt: (bi, g, 0)),
        ),
        out_shape=jax.ShapeDtypeStruct((b, n_blocks, D_MODEL), F32),
        compiler_params=_params(("parallel", "parallel")),
        name="page_means",
    )(page_table, *([cache_kt] * pages_per_step))


def _decode_topk_body(q_ref, cm_ref, seg_ref, idx_ref):
    n_blocks = cm_ref.shape[1]
    hi, lo = _split_bf16(cm_ref[0] * q_ref[0])
    gate = _dot(hi, seg_ref[...]) + _dot(lo, seg_ref[...])
    blk = lax.broadcasted_iota(jnp.int32, gate.shape, 0).astype(F32)
    rows = []
    for _ in range(TOPK):
        mx = jnp.max(gate, axis=0, keepdims=True)
        first = jnp.min(jnp.where(gate == mx, blk, float(n_blocks)), axis=0, keepdims=True)
        rows.append(first)
        gate = jnp.where(blk == first, -jnp.inf, gate)
    idx_ref[0] = jnp.concatenate(rows, axis=0).astype(jnp.int32)


def _decode_topk(q, cmeans):
    b, n_blocks, _ = cmeans.shape
    seg = jnp.asarray((np.arange(D_MODEL)[:, None] // HEAD_DIM == np.arange(N_HEADS)[None, :]).astype(np.float32),
                      dtype=BF16)
    return pl.pallas_call(
        _decode_topk_body,
        grid=(b,),
        in_specs=[pl.BlockSpec((1, 1, D_MODEL), lambda bi: (bi, 0, 0)),
                  pl.BlockSpec((1, n_blocks, D_MODEL), lambda bi: (bi, 0, 0)),
                  _const_spec((D_MODEL, N_HEADS))],
        out_specs=pl.BlockSpec((1, TOPK, N_HEADS), lambda bi: (bi, 0, 0)),
        out_shape=jax.ShapeDtypeStruct((b, TOPK, N_HEADS), jnp.int32),
        compiler_params=_params(("parallel",)),
        name="decode_topk",
    )(q, cmeans, seg)


def _decode_attn_body(pt_ref, idx_ref, tab_ref, q_ref, kn_ref, vn_ref, near_ref, *refs, n_blocks):
    del pt_ref
    n_sel = TOPK * PAGES_PER_BLOCK
    k_pages, v_pages, o_ref = refs[:n_sel], refs[n_sel:2 * n_sel], refs[-1]
    bi = pl.program_id(0)
    h = pl.program_id(1)
    q = q_ref[0, 0]
    far = tab_ref[N_BUCKETS - 1, h]
    s_own = jnp.sum(q * kn_ref[0, 0], axis=0, keepdims=True) * SCALE + tab_ref[0, h]
    scores = []
    m = s_own
    for r in range(TOPK):
        blk = idx_ref[bi, r * N_HEADS + h]
        ok = blk < n_blocks
        adjacent = blk == n_blocks - 1
        for p in range(PAGES_PER_BLOCK):
            kp = k_pages[r * PAGES_PER_BLOCK + p][0, 0]
            s = jnp.sum(kp * q, axis=0, keepdims=True) * SCALE
            bias = jnp.where(adjacent, near_ref[0, :, p * PAGE_SIZE:(p + 1) * PAGE_SIZE], far)
            s = jnp.where(ok, s + bias, NEG)
            scores.append(s)
            m = jnp.maximum(m, jnp.max(s, axis=1, keepdims=True))
    p_own = jnp.exp(s_own - m)
    l = p_own
    acc = p_own * vn_ref[0, 0]
    for s, vp in zip(scores, v_pages):
        p = jnp.exp(s - m)
        l = l + jnp.sum(p, axis=1, keepdims=True)
        acc = acc + jnp.sum(vp[0, 0] * p, axis=1, keepdims=True)
    o_ref[0, 0] = acc / l


def _decode_attn(q, k_new, v_new, cache_kt, cache_vt, page_table, idx, rel_table, near_bias):
    b, n_pages = page_table.shape
    n_blocks = n_pages // PAGES_PER_BLOCK
    tok_spec = pl.BlockSpec((1, 1, HEAD_DIM, 1), lambda bi, h, pt, ix: (bi, h, 0, 0))

    def page_spec(r, p):
        def index(bi, h, pt, ix):
            logical = jnp.minimum(ix[bi, r * N_HEADS + h] * PAGES_PER_BLOCK + p, n_pages - 1)
            return (pt[bi, logical], h, 0, 0)
        return pl.BlockSpec((1, 1, HEAD_DIM, PAGE_SIZE), index)

    page_specs = [page_spec(r, p) for r in range(TOPK) for p in range(PAGES_PER_BLOCK)]
    n_sel = len(page_specs)
    return pl.pallas_call(
        functools.partial(_decode_attn_body, n_blocks=n_blocks),
        grid_spec=pltpu.PrefetchScalarGridSpec(
            num_scalar_prefetch=2,
            grid=(b, N_HEADS),
            in_specs=[pl.BlockSpec(memory_space=pltpu.SMEM), tok_spec, tok_spec, tok_spec,
                      pl.BlockSpec((1, 1, BLOCK), lambda bi, h, pt, ix: (h, 0, 0))] + page_specs + page_specs,
            out_specs=tok_spec,
        ),
        out_shape=jax.ShapeDtypeStruct(q.shape, F32),
        compiler_params=_params(("parallel", "parallel")),
        name="decode_attn",
    )(page_table, idx, rel_table, q, k_new, v_new, near_bias, *([cache_kt] * n_sel), *([cache_vt] * n_sel))


def _pick_tile(m, target):
    return target if m % target == 0 else m


def kernel(x_prompt, x_sample, cache_k, cache_v, page_table, rel_table, g_mix, g_ffn, w_gin, g_gv, w_sp, b_sp,
           w_gout, g_kv, w_k, w_v, g_k, w_q, g_q, w_o, w_f1, w_f3, w_f2):
    bsz, seq, _ = x_prompt.shape
    dec = x_sample.shape[0]
    assert page_table.shape[1] % PAGES_PER_BLOCK == 0 and seq % BLOCK == 0
    cast = lambda w: w.astype(BF16)
    w_gin_b, w_gout_b = cast(w_gin[0]), cast(w_gout[0])
    w_q_b, w_k_b, w_v_b, w_o_b = cast(w_q[0]), cast(w_k), cast(w_v), cast(w_o[0])
    w_f1_b, w_f3_b, w_f2_b = cast(w_f1), cast(w_f3), cast(w_f2)
    qkv_w = (g_mix[1], g_kv, w_q_b, w_k_b, w_v_b, g_q[0], g_k)

    bias_mat, bias_near = _bias_tables(rel_table)

    xp = x_prompt.reshape(bsz * seq, D_MODEL)
    tm = _pick_tile(bsz * seq, 512)
    h = _gmlp(xp, g_mix[0], w_gin_b, g_gv[0], w_sp[0], b_sp[0], w_gout_b, single_token=False, tm=tm)
    h = _ffn(h, g_ffn[0], w_f1_b[0], w_f3_b[0], w_f2_b[0], tm=tm)
    qt, vt, kb, prompt_kt, prompt_vt, means = _qkv_prompt(h.reshape(bsz, seq, D_MODEL), *qkv_w)
    attn = _moba_prompt(qt, kb, vt, means.reshape(bsz, seq // BLOCK, D_MODEL), bias_mat)
    y_prompt = _ffn(h, g_ffn[1], w_f1_b[1], w_f3_b[1], w_f2_b[1], tm=tm,
                    attn=attn.reshape(bsz * seq, D_MODEL), w_o=w_o_b).reshape(bsz, seq, D_MODEL)

    cache_kt = cache_k.transpose(0, 1, 3, 2)
    cache_vt = cache_v.transpose(0, 1, 3, 2)
    xs = x_sample.reshape(dec, D_MODEL)
    hs, v_rows = _gmlp(xs, g_mix[0], w_gin_b, g_gv[0], w_sp[0], b_sp[0], w_gout_b, single_token=True, tm=dec)
    hs = _ffn(hs, g_ffn[0], w_f1_b[0], w_f3_b[0], w_f2_b[0], tm=dec)
    qs, ks, vs = _qkv_decode(hs, *qkv_w)
    cmeans = _cached_block_means(cache_kt, page_table)
    idx = _decode_topk(qs.reshape(dec, 1, D_MODEL), cmeans).reshape(dec, TOPK * N_HEADS)
    as_cols = lambda a: a.reshape(dec, N_HEADS, HEAD_DIM, 1)
    attn_s = _decode_attn(as_cols(qs), as_cols(ks), as_cols(vs), cache_kt, cache_vt, page_table, idx,
                          rel_table, bias_near)
    y_sample = _ffn(hs, g_ffn[1], w_f1_b[1], w_f3_b[1], w_f2_b[1], tm=dec,
                    attn=attn_s.reshape(dec, D_MODEL).astype(BF16), w_o=w_o_b).reshape(dec, 1, D_MODEL)

    as_heads = lambda a: a.reshape(dec, N_HEADS, 1, HEAD_DIM)
    return (y_prompt, y_sample, prompt_kt.transpose(0, 1, 3, 2), prompt_vt.transpose(0, 1, 3, 2),
            as_heads(ks), as_heads(vs), v_rows.reshape(1, dec, 1, D_GATE))
```

```python
import functools
import math

import numpy as np
import jax
import jax.numpy as jnp
from jax import lax
from jax.experimental import pallas as pl
from jax.experimental.pallas import tpu as pltpu

F32 = jnp.float32
BF16 = jnp.bfloat16

D_MODEL = 1024
N_HEADS = 16
HEAD_DIM = 64
CHUNK = 128
N_GROUPS = 8
GROUP_DIM = 128
D_GATE = 1024
BLOCK = 256
TOPK = 3
N_BUCKETS = 32
MAX_DISTANCE = 128
PAGE_SIZE = 128
PAGES_PER_BLOCK = BLOCK // PAGE_SIZE
EPS = 1e-6
NEG = -1e30
SCALE = HEAD_DIM ** -0.5

V7X_LANES = 128
V7X_MXU_DIM = 256
V7X_VMEM_LIMIT = 56 * 1024 * 1024

LOG2E = math.log2(math.e)
FFN_CHUNK = V7X_MXU_DIM
HEADS_PER_LANE_TILE = V7X_LANES // HEAD_DIM
NORM_COLS = V7X_MXU_DIM
MEAN_BLOCKS_PER_STEP = 8
KEY_BLOCKS_PER_STEP = 2
KEY_ROWS_PER_STEP = KEY_BLOCKS_PER_STEP * BLOCK


def _rms(x):
    return x * lax.rsqrt(jnp.mean(x * x, axis=-1, keepdims=True) + EPS)


def _gelu_tanh(x):
    c = math.sqrt(2.0 / math.pi)
    return x * (0.5 * (1.0 + jnp.tanh(c * (x + 0.044715 * (x * x * x)))))


def _dot(a, b):
    return jnp.dot(a, b, preferred_element_type=F32)


def _params(sem, vmem=V7X_VMEM_LIMIT):
    return pltpu.CompilerParams(dimension_semantics=sem, vmem_limit_bytes=vmem)


def _const_spec(shape):
    zeros = (0,) * len(shape)
    return pl.BlockSpec(shape, lambda *_: zeros, pipeline_mode=pl.Buffered(1))


def _gmlp_body(x_ref, gmix_ref, win_ref, ggv_ref, wsp_ref, bsp_ref, wout_ref, *rest, single_token):
    x = x_ref[...]
    hn = (_rms(x) * gmix_ref[...]).astype(BF16)
    uv = _gelu_tanh(_dot(hn, win_ref[...]))
    u = uv[:, :D_GATE]
    v = _rms(uv[:, D_GATE:]) * ggv_ref[...]
    if single_token:
        h_ref, v_ref = rest
        v_ref[...] = v
        mixed = v * wsp_ref[...] + bsp_ref[...]
    else:
        h_ref, mixed_ref = rest
        vb = v.astype(BF16)
        row = lax.broadcasted_iota(jnp.int32, (CHUNK, CHUNK), 0)
        col = lax.broadcasted_iota(jnp.int32, (CHUNK, CHUNK), 1)
        for g in range(N_GROUPS):
            cols = slice(g * GROUP_DIM, (g + 1) * GROUP_DIM)
            w = jnp.where(row >= col, wsp_ref[g], 0.0).astype(BF16)
            for n in range(x.shape[0] // CHUNK):
                rows = slice(n * CHUNK, (n + 1) * CHUNK)
                mixed_ref[rows, cols] = _dot(w, vb[rows, cols]) + bsp_ref[:, cols]
        mixed = mixed_ref[...]
    h_ref[...] = x + _dot((u * mixed).astype(BF16), wout_ref[...])


def _gmlp(x, g_mix, w_in, g_gv, w_sp, b_sp, w_out, *, single_token, tm):
    m = x.shape[0]
    row_spec = pl.BlockSpec((tm, D_MODEL), lambda i: (i, 0))
    if single_token:
        wsp_arg = jnp.repeat(w_sp[:, 0, 0], GROUP_DIM)[None, :]
        bsp_arg = jnp.repeat(b_sp[:, 0], GROUP_DIM)[None, :]
        wsp_spec = _const_spec((1, D_GATE))
        bsp_spec = _const_spec((1, D_GATE))
        out_shape = (jax.ShapeDtypeStruct((m, D_MODEL), F32), jax.ShapeDtypeStruct((m, D_GATE), F32))
        out_specs = (row_spec, pl.BlockSpec((tm, D_GATE), lambda i: (i, 0)))
        scratch = []
    else:
        wsp_arg = w_sp
        bsp_arg = jnp.repeat(b_sp.T, GROUP_DIM, axis=1)
        wsp_spec = _const_spec((N_GROUPS, CHUNK, CHUNK))
        bsp_spec = _const_spec((CHUNK, D_GATE))
        out_shape = jax.ShapeDtypeStruct((m, D_MODEL), F32)
        out_specs = row_spec
        scratch = [pltpu.VMEM((tm, D_GATE), F32)]
    return pl.pallas_call(
        functools.partial(_gmlp_body, single_token=single_token),
        grid=(m // tm,),
        in_specs=[row_spec, _const_spec((1, D_MODEL)), _const_spec((D_MODEL, 2 * D_GATE)),
                  _const_spec((1, D_GATE)), wsp_spec, bsp_spec, _const_spec((D_GATE, D_MODEL))],
        out_specs=out_specs,
        out_shape=out_shape,
        scratch_shapes=scratch,
        compiler_params=_params(("parallel",)),
        name="gmlp_decode" if single_token else "gmlp",
    )(x, g_mix[None, :], w_in, g_gv[None, :], wsp_arg, bsp_arg, w_out)


def _ffn_body(*refs, has_proj):
    if has_proj:
        h_ref, a_ref, wo_ref, g_ref, w1_ref, w3_ref, w2_ref, o_ref = refs
        h = h_ref[...] + _dot(a_ref[...], wo_ref[...])
    else:
        h_ref, g_ref, w1_ref, w3_ref, w2_ref, o_ref = refs
        h = h_ref[...]
    n = (_rms(h) * g_ref[...]).astype(BF16)
    acc = h
    for c in range(w1_ref.shape[1] // FFN_CHUNK):
        cols = slice(c * FFN_CHUNK, (c + 1) * FFN_CHUNK)
        a = _dot(n, w1_ref[:, cols])
        b = _dot(n, w3_ref[:, cols])
        acc = acc + _dot((a * jax.nn.sigmoid(a) * b).astype(BF16), w2_ref[cols, :])
    o_ref[...] = acc


def _ffn(h, g, w1, w3, w2, *, tm, attn=None, w_o=None):
    m = h.shape[0]
    d_ff = w1.shape[1]
    row_spec = pl.BlockSpec((tm, D_MODEL), lambda i: (i, 0))
    has_proj = attn is not None
    args = [h]
    specs = [row_spec]
    if has_proj:
        args += [attn, w_o]
        specs += [row_spec, _const_spec((D_MODEL, D_MODEL))]
    args += [g[None, :], w1, w3, w2]
    specs += [_const_spec((1, D_MODEL)), _const_spec((D_MODEL, d_ff)), _const_spec((D_MODEL, d_ff)),
              _const_spec((d_ff, D_MODEL))]
    return pl.pallas_call(
        functools.partial(_ffn_body, has_proj=has_proj),
        grid=(m // tm,),
        in_specs=specs,
        out_specs=row_spec,
        out_shape=jax.ShapeDtypeStruct((m, D_MODEL), F32),
        compiler_params=_params(("parallel",)),
        name="proj_ffn" if has_proj else "ffn",
    )(*args)


def _qkv_prompt_body(h_ref, gq_ref, gkv_ref, wq_ref, wk_ref, wv_ref, gqh_ref, gkh_ref,
                     qt_ref, vt_ref, kb_ref, k_ref, v_ref, mean_ref):
    n_t = _rms(h_ref[0]).T
    tok = n_t.shape[1]
    hq = (n_t * gq_ref[...]).astype(BF16)
    c = (n_t * gkv_ref[...]).astype(BF16)

    def head_norm(y, g_ref):
        y3 = y.reshape(N_HEADS, HEAD_DIM, tok)
        ms = jnp.mean(y3 * y3, axis=1, keepdims=True)
        return y3 * lax.rsqrt(ms + EPS) * g_ref[...]

    q3 = head_norm(_dot(wq_ref[...], hq), gqh_ref)
    k3 = head_norm(_dot(wk_ref[...], c), gkh_ref)
    v = _dot(wv_ref[...], c)
    qt_ref[0] = (q3 * (SCALE * LOG2E)).reshape(D_MODEL, tok).astype(BF16)
    vt_ref[0, 0] = v.astype(BF16)
    k_ref[0] = k3
    v_ref[0] = v.reshape(N_HEADS, HEAD_DIM, tok)
    k_tok = k3.reshape(D_MODEL, tok).T
    kb_ref[0] = k_tok.astype(BF16)
    mean_ref[0, 0] = jnp.mean(k_tok, axis=0, keepdims=True)


def _qkv_prompt(h, g_q_in, g_kv, w_q, w_k, w_v, g_q, g_k):
    b, s, _ = h.shape
    nb = s // BLOCK
    col = lambda g: g[:, None]
    args = [col(g_q_in), col(g_kv), w_q.T, w_k.T, w_v.T, col(g_q), col(g_k)]
    specs = ([_const_spec((D_MODEL, 1))] * 2 + [_const_spec((D_MODEL, D_MODEL))] * 3
             + [_const_spec((HEAD_DIM, 1))] * 2)
    head_spec = pl.BlockSpec((1, N_HEADS, HEAD_DIM, BLOCK), lambda bi, i: (bi, 0, 0, i))
    head_shape = jax.ShapeDtypeStruct((b, N_HEADS, HEAD_DIM, s), F32)
    return pl.pallas_call(
        _qkv_prompt_body,
        grid=(b, nb),
        in_specs=[pl.BlockSpec((1, BLOCK, D_MODEL), lambda bi, i: (bi, i, 0))] + specs,
        out_specs=(pl.BlockSpec((1, D_MODEL, BLOCK), lambda bi, i: (bi, 0, i)),
                   pl.BlockSpec((1, 1, D_MODEL, BLOCK), lambda bi, i: (bi, i, 0, 0)),
                   pl.BlockSpec((1, BLOCK, D_MODEL), lambda bi, i: (bi, i, 0)),
                   head_spec, head_spec,
                   pl.BlockSpec((1, 1, 1, D_MODEL), lambda bi, i: (bi, i, 0, 0))),
        out_shape=(jax.ShapeDtypeStruct((b, D_MODEL, s), BF16),
                   jax.ShapeDtypeStruct((b, nb, D_MODEL, BLOCK), BF16),
                   jax.ShapeDtypeStruct((b, s, D_MODEL), BF16),
                   head_shape, head_shape,
                   jax.ShapeDtypeStruct((b, nb, 1, D_MODEL), F32)),
        compiler_params=_params(("parallel", "parallel")),
        name="qkv",
    )(h, *args)


def _qkv_decode_body(h_ref, gq_ref, gkv_ref, wq_ref, wk_ref, wv_ref, gqh_ref, gkh_ref, grp_ref,
                     q_ref, k_ref, v_ref):
    n = _rms(h_ref[...])
    hq = (n * gq_ref[...]).astype(BF16)
    c = (n * gkv_ref[...]).astype(BF16)

    def head_norm(y, g_ref):
        pieces = []
        for cb in range(D_MODEL // NORM_COLS):
            ys = y[:, cb * NORM_COLS:(cb + 1) * NORM_COLS]
            ms = _dot((ys * ys).astype(BF16), grp_ref[...])
            pieces.append(ys * lax.rsqrt(ms + EPS))
        return jnp.concatenate(pieces, axis=1) * g_ref[...]

    q_ref[...] = head_norm(_dot(hq, wq_ref[...]), gqh_ref)
    k_ref[...] = head_norm(_dot(c, wk_ref[...]), gkh_ref)
    v_ref[...] = _dot(c, wv_ref[...])


def _qkv_decode(h, g_q_in, g_kv, w_q, w_k, w_v, g_q, g_k):
    m = h.shape[0]
    head = np.arange(NORM_COLS) // HEAD_DIM
    grp = jnp.asarray((head[:, None] == head[None, :]).astype(np.float32) / HEAD_DIM, dtype=BF16)
    args = [g_q_in[None, :], g_kv[None, :], w_q, w_k, w_v,
            jnp.tile(g_q, N_HEADS)[None, :], jnp.tile(g_k, N_HEADS)[None, :], grp]
    specs = ([_const_spec((1, D_MODEL))] * 2 + [_const_spec((D_MODEL, D_MODEL))] * 3
             + [_const_spec((1, D_MODEL))] * 2 + [_const_spec((NORM_COLS, NORM_COLS))])
    row_spec = pl.BlockSpec((m, D_MODEL), lambda i: (0, 0))
    shape = jax.ShapeDtypeStruct((m, D_MODEL), F32)
    return pl.pallas_call(
        _qkv_decode_body,
        grid=(1,),
        in_specs=[row_spec] + specs,
        out_specs=(row_spec, row_spec, row_spec),
        out_shape=(shape, shape, shape),
        compiler_params=_params(("arbitrary",)),
        name="qkv_decode",
    )(h, *args)


def _bucket_of_distance(n):
    n = np.asarray(n, dtype=np.int64)
    max_exact = N_BUCKETS // 2
    nf = np.maximum(n, 1).astype(np.float32)
    large = max_exact + (np.log(nf / np.float32(max_exact)) / np.float32(math.log(MAX_DISTANCE / max_exact))
                         * np.float32(N_BUCKETS - max_exact)).astype(np.int32)
    large = np.minimum(large, N_BUCKETS - 1)
    return np.where(n < max_exact, n, large).astype(np.int32)


def _bias_body(tab_ref, bm_ref, bv_ref, mat_ref, vec_ref):
    h = pl.program_id(0)
    mat = jnp.full(bm_ref.shape, NEG, F32)
    vec = jnp.full(bv_ref.shape, NEG, F32)
    for bkt in range(N_BUCKETS):
        val = tab_ref[bkt, h]
        mat = jnp.where(bm_ref[...] == bkt, val * LOG2E, mat)
        vec = jnp.where(bv_ref[...] == bkt, val, vec)
    mat_ref[0] = mat
    vec_ref[0] = vec


def _bias_tables(rel_table):
    key = np.arange(BLOCK)[:, None]
    qry = np.arange(BLOCK)[None, :]
    own = np.where(qry >= key, _bucket_of_distance(qry - key), -1)
    adj = _bucket_of_distance(BLOCK + qry - key)
    far = np.full((BLOCK, BLOCK), _bucket_of_distance(2 * BLOCK), np.int32)
    bm = jnp.asarray(np.stack([far, adj, own]).astype(np.int32))
    bv = jnp.asarray(_bucket_of_distance(BLOCK - np.arange(BLOCK))[None, :].astype(np.int32))
    return pl.pallas_call(
        _bias_body,
        grid=(N_HEADS,),
        in_specs=[pl.BlockSpec(memory_space=pltpu.SMEM), _const_spec((3, BLOCK, BLOCK)), _const_spec((1, BLOCK))],
        out_specs=(pl.BlockSpec((1, 3, BLOCK, BLOCK), lambda h: (h, 0, 0, 0)),
                   pl.BlockSpec((1, 1, BLOCK), lambda h: (h, 0, 0))),
        out_shape=(jax.ShapeDtypeStruct((N_HEADS, 3, BLOCK, BLOCK), F32),
                   jax.ShapeDtypeStruct((N_HEADS, 1, BLOCK), F32)),
        compiler_params=_params(("arbitrary",)),
        name="rel_bias",
    )(rel_table, bm, bv)


def _top_blocks_penalty(gate, n_valid, own):
    nb = gate.shape[0]
    blk = lax.broadcasted_iota(jnp.int32, gate.shape, 0).astype(F32)
    valid = blk < n_valid
    g = jnp.where(valid, gate, NEG)
    sel = jnp.zeros(gate.shape, jnp.bool_)
    for _ in range(TOPK):
        mx = jnp.max(g, axis=0, keepdims=True)
        first = jnp.min(jnp.where(g == mx, blk, float(nb)), axis=0, keepdims=True)
        pick = blk == first
        sel = jnp.logical_or(sel, pick)
        g = jnp.where(pick, -jnp.inf, g)
    sel = jnp.logical_or(jnp.logical_and(sel, valid), blk == own)
    return jnp.where(sel, 0.0, NEG)


def _moba_body(qt_ref, k_ref, vt_ref, mean_ref, bias_ref, o_ref, qm_ref, pen_ref, sa_ref, sb_ref):
    i = pl.program_id(2)
    qt = qt_ref[0]
    feat = lax.broadcasted_iota(jnp.int32, qt.shape, 0)
    mean = mean_ref[0]
    mean_hi = mean.astype(BF16)
    mean_lo = (mean - mean_hi.astype(F32)).astype(BF16)
    i_f = i.astype(F32)
    for t in range(HEADS_PER_LANE_TILE):
        in_head = jnp.logical_and(feat >= t * HEAD_DIM, feat < (t + 1) * HEAD_DIM)
        qt_t = jnp.where(in_head, qt, jnp.zeros_like(qt))
        qm_ref[t] = qt_t
        gate = _dot(mean_hi, qt_t) + _dot(mean_lo, qt_t)
        pen_ref[t] = _top_blocks_penalty(gate, i_f, i_f)

    def rows(j):
        return pl.ds(pl.multiple_of(j * BLOCK, BLOCK), BLOCK)

    def head_rows(t):
        return slice(t * HEAD_DIM, (t + 1) * HEAD_DIM)

    heads = range(HEADS_PER_LANE_TILE)
    subs = range(KEY_BLOCKS_PER_STEP)

    def col_max(x):
        return jnp.max(x, axis=0, keepdims=True)


    def score_near(buf):
        j_adj = jnp.maximum(i - 1, 0)
        no_adj = jnp.where(i >= 1, 0.0, NEG)
        k_adj = k_ref[0, rows(j_adj), :]
        k_own = k_ref[0, rows(i), :]
        s_adj = [_dot(k_adj, qm_ref[t]) for t in heads]
        s_own = [_dot(k_own, qm_ref[t]) for t in heads]
        meta = []
        for t in heads:
            tiles = [s_adj[t] + bias_ref[t, 1], s_own[t] + bias_ref[t, 2]]
            offs = [pen_ref[t, pl.ds(j_adj, 1), :] + no_adj, jnp.zeros((1, BLOCK), F32)]
            for sub in subs:
                buf[t, sub] = tiles[sub]
            meta.append((offs, jnp.maximum(col_max(tiles[0]) + offs[0], col_max(tiles[1]))))
        return tuple(meta), (j_adj, i)

    n_far = jnp.maximum(i - 1, 0)

    def score_far(c, buf):
        kc = k_ref[0, pl.ds(pl.multiple_of(c * KEY_ROWS_PER_STEP, KEY_ROWS_PER_STEP), KEY_ROWS_PER_STEP), :]
        blocks = tuple(c * KEY_BLOCKS_PER_STEP + sub for sub in subs)
        scores = [_dot(kc, qm_ref[t]) for t in heads]
        meta = []
        for t in heads:
            far_bias = bias_ref[t, 0, 0:1, 0:1]
            tiles = [scores[t][sub * BLOCK:(sub + 1) * BLOCK] for sub in subs]
            offs = [pen_ref[t, pl.ds(blocks[sub], 1), :] + far_bias + jnp.where(blocks[sub] < n_far, 0.0, NEG)
                    for sub in subs]
            cm = col_max(tiles[0]) + offs[0]
            for sub in subs[1:]:
                cm = jnp.maximum(cm, col_max(tiles[sub]) + offs[sub])
            for sub in subs:
                buf[t, sub] = tiles[sub]
            meta.append((offs, cm))
        return tuple(meta), blocks

    def fold(state, buf, meta, blocks):
        new = []
        for t in heads:
            m, l, acc = state[t]
            offs, cm = meta[t]
            m_new = jnp.maximum(m, cm)
            alpha = jnp.exp2(m - m_new)
            l = alpha * l
            acc = alpha * acc
            for sub in subs:
                p = jnp.exp2(buf[t, sub] - (m_new - offs[sub]))
                l = l + jnp.sum(p, axis=0, keepdims=True)
                acc = acc + _dot(vt_ref[0, blocks[sub], head_rows(t), :], p.astype(BF16))
            new.append((m_new, l, acc))
        return tuple(new)

    def step(c2, carry):
        state, meta, blocks = carry
        meta_b, blocks_b = score_far(2 * c2, sb_ref)
        state = fold(state, sa_ref, meta, blocks)
        meta_a, blocks_a = score_far(2 * c2 + 1, sa_ref)
        state = fold(state, sb_ref, meta_b, blocks_b)
        return state, meta_a, blocks_a

    state = tuple((jnp.full((1, BLOCK), -jnp.inf, F32), jnp.zeros((1, BLOCK), F32),
                   jnp.zeros((HEAD_DIM, BLOCK), F32)) for _ in heads)
    n_units = (n_far + KEY_BLOCKS_PER_STEP - 1) // KEY_BLOCKS_PER_STEP
    meta, blocks = score_near(sa_ref)
    state, meta, blocks = lax.fori_loop(0, (n_units + 1) // 2, step, (state, meta, blocks))
    state = fold(state, sa_ref, meta, blocks)
    ot = jnp.concatenate([acc / l for (_, l, acc) in state], axis=0)
    o_ref[0] = ot.T.astype(o_ref.dtype)


def _moba_prompt(qt, kb, vt, means, bias_mat):
    b, s, _ = kb.shape
    nb = s // BLOCK
    assert pl.cdiv(pl.cdiv(max(nb - 2, 0), KEY_BLOCKS_PER_STEP), 2) * 2 * KEY_BLOCKS_PER_STEP <= nb
    hp = D_MODEL // V7X_LANES
    return pl.pallas_call(
        _moba_body,
        grid=(b, hp, nb),
        in_specs=[pl.BlockSpec((1, V7X_LANES, BLOCK), lambda bi, p, i: (bi, p, i)),
                  pl.BlockSpec((1, s, V7X_LANES), lambda bi, p, i: (bi, 0, p)),
                  pl.BlockSpec((1, nb, V7X_LANES, BLOCK), lambda bi, p, i: (bi, 0, p, 0)),
                  pl.BlockSpec((1, nb, V7X_LANES), lambda bi, p, i: (bi, 0, p)),
                  pl.BlockSpec((HEADS_PER_LANE_TILE, 3, BLOCK, BLOCK), lambda bi, p, i: (p, 0, 0, 0))],
        out_specs=pl.BlockSpec((1, BLOCK, V7X_LANES), lambda bi, p, i: (bi, i, p)),
        out_shape=jax.ShapeDtypeStruct((b, s, D_MODEL), BF16),
        scratch_shapes=[pltpu.VMEM((HEADS_PER_LANE_TILE, V7X_LANES, BLOCK), BF16),
                        pltpu.VMEM((HEADS_PER_LANE_TILE, nb, BLOCK), F32)]
        + [pltpu.VMEM((HEADS_PER_LANE_TILE, KEY_BLOCKS_PER_STEP, BLOCK, BLOCK), F32)] * 2,
        compiler_params=_params(("parallel", "parallel", "arbitrary")),
        name="moba_prompt",
    )(qt, kb, vt, means, bias_mat)


def _split_bf16(x):
    hi = x.astype(BF16)
    return hi, (x - hi.astype(F32)).astype(BF16)


def _page_mean_body(pt_ref, *refs):
    del pt_ref
    pages, o_ref = refs[:-1], refs[-1]
    ones = jnp.ones((8, PAGE_SIZE), BF16)
    contract_minor = (((1,), (1,)), ((), ()))
    rows = []
    for t in range(len(pages) // PAGES_PER_BLOCK):
        tot = pages[PAGES_PER_BLOCK * t][0]
        for p in range(1, PAGES_PER_BLOCK):
            tot = tot + pages[PAGES_PER_BLOCK * t + p][0]
        hi, lo = _split_bf16(tot.reshape(D_MODEL, PAGE_SIZE))
        sums = (lax.dot_general(ones, hi, contract_minor, preferred_element_type=F32)
                + lax.dot_general(ones, lo, contract_minor, preferred_element_type=F32))
        rows.append(sums[0:1])
    o_ref[0] = jnp.concatenate(rows, axis=0) * (1.0 / BLOCK)


def _cached_block_means(cache_kt, page_table):
    b, n_pages = page_table.shape
    n_blocks = n_pages // PAGES_PER_BLOCK
    steps = n_blocks // MEAN_BLOCKS_PER_STEP
    pages_per_step = MEAN_BLOCKS_PER_STEP * PAGES_PER_BLOCK

    def page_spec(t):
        return pl.BlockSpec((1, N_HEADS, HEAD_DIM, PAGE_SIZE),
                            lambda bi, g, pt: (pt[bi, g * pages_per_step + t], 0, 0, 0))

    return pl.pallas_call(
        _page_mean_body,
        grid_spec=pltpu.PrefetchScalarGridSpec(
            num_scalar_prefetch=1,
            grid=(b, steps),
            in_specs=[page_spec(t) for t in range(pages_per_step)],
            out_specs=pl.BlockSpec((1, MEAN_BLOCKS_PER_STEP, D_MODEL), lambda bi, g, pt: (bi, g, 0)),
        ),
        out_shape=jax.ShapeDtypeStruct((b, n_blocks, D_MODEL), F32),
        compiler_params=_params(("parallel", "parallel")),
        name="page_means",
    )(page_table, *([cache_kt] * pages_per_step))


def _decode_topk_body(q_ref, cm_ref, seg_ref, idx_ref):
    n_blocks = cm_ref.shape[1]
    hi, lo = _split_bf16(cm_ref[0] * q_ref[0])
    gate = _dot(hi, seg_ref[...]) + _dot(lo, seg_ref[...])
    blk = lax.broadcasted_iota(jnp.int32, gate.shape, 0).astype(F32)
    rows = []
    for _ in range(TOPK):
        mx = jnp.max(gate, axis=0, keepdims=True)
        first = jnp.min(jnp.where(gate == mx, blk, float(n_blocks)), axis=0, keepdims=True)
        rows.append(first)
        gate = jnp.where(blk == first, -jnp.inf, gate)
    idx_ref[0] = jnp.concatenate(rows, axis=0).astype(jnp.int32)


def _decode_topk(q, cmeans):
    b, n_blocks, _ = cmeans.shape
    seg = jnp.asarray((np.arange(D_MODEL)[:, None] // HEAD_DIM == np.arange(N_HEADS)[None, :]).astype(np.float32),
                      dtype=BF16)
    return pl.pallas_call(
        _decode_topk_body,
        grid=(b,),
        in_specs=[pl.BlockSpec((1, 1, D_MODEL), lambda bi: (bi, 0, 0)),
                  pl.BlockSpec((1, n_blocks, D_MODEL), lambda bi: (bi, 0, 0)),
                  _const_spec((D_MODEL, N_HEADS))],
        out_specs=pl.BlockSpec((1, TOPK, N_HEADS), lambda bi: (bi, 0, 0)),
        out_shape=jax.ShapeDtypeStruct((b, TOPK, N_HEADS), jnp.int32),
        compiler_params=_params(("parallel",)),
        name="decode_topk",
    )(q, cmeans, seg)


def _decode_attn_body(pt_ref, idx_ref, tab_ref, q_ref, kn_ref, vn_ref, near_ref, ck_ref, cv_ref, o_ref,
                      kbuf, vbuf, sem, *, n_blocks, n_pages):
    bi = pl.program_id(0)

    def tile(h, r, p):
        return (h * TOPK + r) * PAGES_PER_BLOCK + p

    def gather(row, slot, action):
        def per_head(h, carry):
            for r in range(TOPK):
                blk = idx_ref[row, r * N_HEADS + h]
                for p in range(PAGES_PER_BLOCK):
                    logical = jnp.minimum(blk * PAGES_PER_BLOCK + p, n_pages - 1)
                    phys = pt_ref[row, logical]
                    n = tile(h, r, p)
                    action(pltpu.make_async_copy(ck_ref.at[phys, h], kbuf.at[slot, n], sem.at[0, slot]))
                    action(pltpu.make_async_copy(cv_ref.at[phys, h], vbuf.at[slot, n], sem.at[1, slot]))
            return carry
        lax.fori_loop(0, N_HEADS, per_head, 0)

    slot = bi % 2

    @pl.when(bi == 0)
    def _():
        gather(bi, slot, lambda c: c.start())

    @pl.when(bi + 1 < pl.num_programs(0))
    def _():
        gather(bi + 1, 1 - slot, lambda c: c.start())

    gather(bi, slot, lambda c: c.wait())

    for h in range(N_HEADS):
        q = q_ref[0, :, h:h + 1]
        far = tab_ref[N_BUCKETS - 1, h]
        s_own = jnp.sum(q * kn_ref[0, :, h:h + 1], axis=0, keepdims=True) * SCALE + tab_ref[0, h]
        scores = []
        m = s_own
        for r in range(TOPK):
            blk = idx_ref[bi, r * N_HEADS + h]
            ok = blk < n_blocks
            adjacent = blk == n_blocks - 1
            for p in range(PAGES_PER_BLOCK):
                kp = kbuf[slot, tile(h, r, p)]
                s = jnp.sum(kp * q, axis=0, keepdims=True) * SCALE
                bias = jnp.where(adjacent, near_ref[h, :, p * PAGE_SIZE:(p + 1) * PAGE_SIZE], far)
                s = jnp.where(ok, s + bias, NEG)
                scores.append(s)
                m = jnp.maximum(m, jnp.max(s, axis=1, keepdims=True))
        p_own = jnp.exp(s_own - m)
        l = p_own
        acc = p_own * vn_ref[0, :, h:h + 1]
        for n, s in enumerate(scores):
            p = jnp.exp(s - m)
            l = l + jnp.sum(p, axis=1, keepdims=True)
            acc = acc + jnp.sum(vbuf[slot, h * len(scores) + n] * p, axis=1, keepdims=True)
        o_ref[0, :, h:h + 1] = acc / l


def _decode_attn(q, k_new, v_new, cache_kt, cache_vt, page_table, idx, rel_table, near_bias):
    b, n_pages = page_table.shape
    n_blocks = n_pages // PAGES_PER_BLOCK
    n_tiles = N_HEADS * TOPK * PAGES_PER_BLOCK
    tok_spec = pl.BlockSpec((1, HEAD_DIM, N_HEADS), lambda bi, pt, ix: (bi, 0, 0))
    return pl.pallas_call(
        functools.partial(_decode_attn_body, n_blocks=n_blocks, n_pages=n_pages),
        grid_spec=pltpu.PrefetchScalarGridSpec(
            num_scalar_prefetch=2,
            grid=(b,),
            in_specs=[pl.BlockSpec(memory_space=pltpu.SMEM), tok_spec, tok_spec, tok_spec,
                      pl.BlockSpec((N_HEADS, 1, BLOCK), lambda bi, pt, ix: (0, 0, 0)),
                      pl.BlockSpec(memory_space=pl.ANY), pl.BlockSpec(memory_space=pl.ANY)],
            out_specs=tok_spec,
            scratch_shapes=[pltpu.VMEM((2, n_tiles, HEAD_DIM, PAGE_SIZE), F32),
                            pltpu.VMEM((2, n_tiles, HEAD_DIM, PAGE_SIZE), F32),
                            pltpu.SemaphoreType.DMA((2, 2))],
        ),
        out_shape=jax.ShapeDtypeStruct(q.shape, F32),
        compiler_params=_params(("arbitrary",)),
        name="decode_attn",
    )(page_table, idx, rel_table, q, k_new, v_new, near_bias, cache_kt, cache_vt)


def _pick_tile(m, target):
    return target if m % target == 0 else m


def kernel(x_prompt, x_sample, cache_k, cache_v, page_table, rel_table, g_mix, g_ffn, w_gin, g_gv, w_sp, b_sp,
           w_gout, g_kv, w_k, w_v, g_k, w_q, g_q, w_o, w_f1, w_f3, w_f2):
    bsz, seq, _ = x_prompt.shape
    dec = x_sample.shape[0]
    assert page_table.shape[1] % PAGES_PER_BLOCK == 0 and seq % BLOCK == 0
    cast = lambda w: w.astype(BF16)
    w_gin_b, w_gout_b = cast(w_gin[0]), cast(w_gout[0])
    w_q_b, w_k_b, w_v_b, w_o_b = cast(w_q[0]), cast(w_k), cast(w_v), cast(w_o[0])
    w_f1_b, w_f3_b, w_f2_b = cast(w_f1), cast(w_f3), cast(w_f2)
    qkv_w = (g_mix[1], g_kv, w_q_b, w_k_b, w_v_b, g_q[0], g_k)

    bias_mat, bias_near = _bias_tables(rel_table)

    xp = x_prompt.reshape(bsz * seq, D_MODEL)
    tm = _pick_tile(bsz * seq, 512)
    h = _gmlp(xp, g_mix[0], w_gin_b, g_gv[0], w_sp[0], b_sp[0], w_gout_b, single_token=False, tm=tm)
    h = _ffn(h, g_ffn[0], w_f1_b[0], w_f3_b[0], w_f2_b[0], tm=tm)
    qt, vt, kb, prompt_kt, prompt_vt, means = _qkv_prompt(h.reshape(bsz, seq, D_MODEL), *qkv_w)
    attn = _moba_prompt(qt, kb, vt, means.reshape(bsz, seq // BLOCK, D_MODEL), bias_mat)
    y_prompt = _ffn(h, g_ffn[1], w_f1_b[1], w_f3_b[1], w_f2_b[1], tm=tm,
                    attn=attn.reshape(bsz * seq, D_MODEL), w_o=w_o_b).reshape(bsz, seq, D_MODEL)

    cache_kt = cache_k.transpose(0, 1, 3, 2)
    cache_vt = cache_v.transpose(0, 1, 3, 2)
    xs = x_sample.reshape(dec, D_MODEL)
    hs, v_rows = _gmlp(xs, g_mix[0], w_gin_b, g_gv[0], w_sp[0], b_sp[0], w_gout_b, single_token=True, tm=dec)
    hs = _ffn(hs, g_ffn[0], w_f1_b[0], w_f3_b[0], w_f2_b[0], tm=dec)
    qs, ks, vs = _qkv_decode(hs, *qkv_w)
    cmeans = _cached_block_means(cache_kt, page_table)
    idx = _decode_topk(qs.reshape(dec, 1, D_MODEL), cmeans).reshape(dec, TOPK * N_HEADS)
    as_cols = lambda a: a.reshape(dec, N_HEADS, HEAD_DIM).transpose(0, 2, 1)
    attn_s = _decode_attn(as_cols(qs), as_cols(ks), as_cols(vs), cache_kt, cache_vt, page_table, idx,
                          rel_table, bias_near)
    attn_s = attn_s.transpose(0, 2, 1).reshape(dec, D_MODEL).astype(BF16)
    y_sample = _ffn(hs, g_ffn[1], w_f1_b[1], w_f3_b[1], w_f2_b[1], tm=dec,
                    attn=attn_s, w_o=w_o_b).reshape(dec, 1, D_MODEL)

    as_heads = lambda a: a.reshape(dec, N_HEADS, 1, HEAD_DIM)
    return (y_prompt, y_sample, prompt_kt.transpose(0, 1, 3, 2), prompt_vt.transpose(0, 1, 3, 2),
            as_heads(ks), as_heads(vs), v_rows.reshape(1, dec, 1, D_GATE))
```

```python
import functools
import math

import numpy as np
import jax
import jax.numpy as jnp
from jax import lax
from jax.experimental import pallas as pl
from jax.experimental.pallas import tpu as pltpu

F32 = jnp.float32
BF16 = jnp.bfloat16

D_MODEL = 1024
N_HEADS = 16
HEAD_DIM = 64
CHUNK = 128
N_GROUPS = 8
GROUP_DIM = 128
D_GATE = 1024
BLOCK = 256
TOPK = 3
N_BUCKETS = 32
MAX_DISTANCE = 128
PAGE_SIZE = 128
PAGES_PER_BLOCK = BLOCK // PAGE_SIZE
EPS = 1e-6
NEG = -1e30
SCALE = HEAD_DIM ** -0.5

V7X_LANES = 128
V7X_MXU_DIM = 256
V7X_VMEM_LIMIT = 56 * 1024 * 1024

LOG2E = math.log2(math.e)
FFN_CHUNK = V7X_MXU_DIM
HEADS_PER_LANE_TILE = V7X_LANES // HEAD_DIM
NORM_COLS = V7X_MXU_DIM
MEAN_BLOCKS_PER_STEP = 8
KEY_BLOCKS_PER_STEP = 2
KEY_ROWS_PER_STEP = KEY_BLOCKS_PER_STEP * BLOCK
MOBA_HEADS = 4


def _rms(x):
    return x * lax.rsqrt(jnp.mean(x * x, axis=-1, keepdims=True) + EPS)


def _gelu_tanh(x):
    c = math.sqrt(2.0 / math.pi)
    return x * (0.5 * (1.0 + jnp.tanh(c * (x + 0.044715 * (x * x * x)))))


def _dot(a, b):
    return jnp.dot(a, b, preferred_element_type=F32)


def _split_bf16(x):
    hi = x.astype(BF16)
    return hi, (x - hi.astype(F32)).astype(BF16)


def _params(sem, vmem=V7X_VMEM_LIMIT):
    return pltpu.CompilerParams(dimension_semantics=sem, vmem_limit_bytes=vmem)


def _const_spec(shape):
    zeros = (0,) * len(shape)
    return pl.BlockSpec(shape, lambda *_: zeros, pipeline_mode=pl.Buffered(1))


def _gmlp_body(x_ref, gmix_ref, win_ref, ggv_ref, wsp_ref, bsp_ref, wout_ref, *rest, single_token):
    x = x_ref[...]
    hn = (_rms(x) * gmix_ref[...]).astype(BF16)
    uv = _gelu_tanh(_dot(hn, win_ref[...]))
    u = uv[:, :D_GATE]
    v = _rms(uv[:, D_GATE:]) * ggv_ref[...]
    if single_token:
        h_ref, v_ref = rest
        v_ref[...] = v
        mixed = v * wsp_ref[...] + bsp_ref[...]
    else:
        h_ref, mixed_ref = rest
        vb = v.astype(BF16)
        row = lax.broadcasted_iota(jnp.int32, (CHUNK, CHUNK), 0)
        col = lax.broadcasted_iota(jnp.int32, (CHUNK, CHUNK), 1)
        for g in range(N_GROUPS):
            cols = slice(g * GROUP_DIM, (g + 1) * GROUP_DIM)
            w = jnp.where(row >= col, wsp_ref[g], 0.0).astype(BF16)
            for n in range(x.shape[0] // CHUNK):
                rows = slice(n * CHUNK, (n + 1) * CHUNK)
                mixed_ref[rows, cols] = _dot(w, vb[rows, cols]) + bsp_ref[:, cols]
        mixed = mixed_ref[...]
    h_ref[...] = x + _dot((u * mixed).astype(BF16), wout_ref[...])


def _gmlp(x, g_mix, w_in, g_gv, w_sp, b_sp, w_out, *, single_token, tm):
    m = x.shape[0]
    row_spec = pl.BlockSpec((tm, D_MODEL), lambda i: (i, 0))
    if single_token:
        wsp_arg = jnp.repeat(w_sp[:, 0, 0], GROUP_DIM)[None, :]
        bsp_arg = jnp.repeat(b_sp[:, 0], GROUP_DIM)[None, :]
        wsp_spec = _const_spec((1, D_GATE))
        bsp_spec = _const_spec((1, D_GATE))
        out_shape = (jax.ShapeDtypeStruct((m, D_MODEL), F32), jax.ShapeDtypeStruct((m, D_GATE), F32))
        out_specs = (row_spec, pl.BlockSpec((tm, D_GATE), lambda i: (i, 0)))
        scratch = []
    else:
        wsp_arg = w_sp
        bsp_arg = jnp.repeat(b_sp.T, GROUP_DIM, axis=1)
        wsp_spec = _const_spec((N_GROUPS, CHUNK, CHUNK))
        bsp_spec = _const_spec((CHUNK, D_GATE))
        out_shape = jax.ShapeDtypeStruct((m, D_MODEL), F32)
        out_specs = row_spec
        scratch = [pltpu.VMEM((tm, D_GATE), F32)]
    return pl.pallas_call(
        functools.partial(_gmlp_body, single_token=single_token),
        grid=(m // tm,),
        in_specs=[row_spec, _const_spec((1, D_MODEL)), _const_spec((D_MODEL, 2 * D_GATE)),
                  _const_spec((1, D_GATE)), wsp_spec, bsp_spec, _const_spec((D_GATE, D_MODEL))],
        out_specs=out_specs,
        out_shape=out_shape,
        scratch_shapes=scratch,
        compiler_params=_params(("parallel",)),
        name="gmlp_decode" if single_token else "gmlp",
    )(x, g_mix[None, :], w_in, g_gv[None, :], wsp_arg, bsp_arg, w_out)


def _ffn_body(*refs, has_proj):
    if has_proj:
        h_ref, a_ref, wo_ref, g_ref, w1_ref, w3_ref, w2_ref, o_ref = refs
        h = h_ref[...] + _dot(a_ref[...], wo_ref[...])
    else:
        h_ref, g_ref, w1_ref, w3_ref, w2_ref, o_ref = refs
        h = h_ref[...]
    n = (_rms(h) * g_ref[...]).astype(BF16)
    acc = h
    for c in range(w1_ref.shape[1] // FFN_CHUNK):
        cols = slice(c * FFN_CHUNK, (c + 1) * FFN_CHUNK)
        a = _dot(n, w1_ref[:, cols])
        b = _dot(n, w3_ref[:, cols])
        acc = acc + _dot((a * jax.nn.sigmoid(a) * b).astype(BF16), w2_ref[cols, :])
    o_ref[...] = acc


def _ffn(h, g, w1, w3, w2, *, tm, attn=None, w_o=None):
    m = h.shape[0]
    d_ff = w1.shape[1]
    row_spec = pl.BlockSpec((tm, D_MODEL), lambda i: (i, 0))
    has_proj = attn is not None
    args = [h]
    specs = [row_spec]
    if has_proj:
        args += [attn, w_o]
        specs += [row_spec, _const_spec((D_MODEL, D_MODEL))]
    args += [g[None, :], w1, w3, w2]
    specs += [_const_spec((1, D_MODEL)), _const_spec((D_MODEL, d_ff)), _const_spec((D_MODEL, d_ff)),
              _const_spec((d_ff, D_MODEL))]
    return pl.pallas_call(
        functools.partial(_ffn_body, has_proj=has_proj),
        grid=(m // tm,),
        in_specs=specs,
        out_specs=row_spec,
        out_shape=jax.ShapeDtypeStruct((m, D_MODEL), F32),
        compiler_params=_params(("parallel",)),
        name="proj_ffn" if has_proj else "ffn",
    )(*args)


def _qkv_prompt_body(h_ref, gq_ref, gkv_ref, wq_ref, wk_ref, wv_ref, gqh_ref, gkh_ref,
                     qt_ref, vt_ref, kb_ref, k_ref, v_ref, mean_ref):
    n_t = _rms(h_ref[0]).T
    tok = n_t.shape[1]
    hq = (n_t * gq_ref[...]).astype(BF16)
    c = (n_t * gkv_ref[...]).astype(BF16)

    def head_norm(y, g_ref):
        y3 = y.reshape(N_HEADS, HEAD_DIM, tok)
        ms = jnp.mean(y3 * y3, axis=1, keepdims=True)
        return y3 * lax.rsqrt(ms + EPS) * g_ref[...]

    q3 = head_norm(_dot(wq_ref[...], hq), gqh_ref)
    k3 = head_norm(_dot(wk_ref[...], c), gkh_ref)
    v = _dot(wv_ref[...], c)
    qt_ref[0] = (q3 * (SCALE * LOG2E)).reshape(D_MODEL, tok).astype(BF16)
    vt_ref[0, 0] = v.astype(BF16)
    k_ref[0] = k3
    v_ref[0] = v.reshape(N_HEADS, HEAD_DIM, tok)
    k_tok = k3.reshape(D_MODEL, tok).T
    kb_ref[0] = k_tok.astype(BF16)
    mean_ref[0, 0] = jnp.mean(k_tok, axis=0, keepdims=True)


def _qkv_prompt(h, g_q_in, g_kv, w_q, w_k, w_v, g_q, g_k):
    b, s, _ = h.shape
    nb = s // BLOCK
    col = lambda g: g[:, None]
    args = [col(g_q_in), col(g_kv), w_q.T, w_k.T, w_v.T, col(g_q), col(g_k)]
    specs = ([_const_spec((D_MODEL, 1))] * 2 + [_const_spec((D_MODEL, D_MODEL))] * 3
             + [_const_spec((HEAD_DIM, 1))] * 2)
    head_spec = pl.BlockSpec((1, N_HEADS, HEAD_DIM, BLOCK), lambda bi, i: (bi, 0, 0, i))
    head_shape = jax.ShapeDtypeStruct((b, N_HEADS, HEAD_DIM, s), F32)
    return pl.pallas_call(
        _qkv_prompt_body,
        grid=(b, nb),
        in_specs=[pl.BlockSpec((1, BLOCK, D_MODEL), lambda bi, i: (bi, i, 0))] + specs,
        out_specs=(pl.BlockSpec((1, D_MODEL, BLOCK), lambda bi, i: (bi, 0, i)),
                   pl.BlockSpec((1, 1, D_MODEL, BLOCK), lambda bi, i: (bi, i, 0, 0)),
                   pl.BlockSpec((1, BLOCK, D_MODEL), lambda bi, i: (bi, i, 0)),
                   head_spec, head_spec,
                   pl.BlockSpec((1, 1, 1, D_MODEL), lambda bi, i: (bi, i, 0, 0))),
        out_shape=(jax.ShapeDtypeStruct((b, D_MODEL, s), BF16),
                   jax.ShapeDtypeStruct((b, nb, D_MODEL, BLOCK), BF16),
                   jax.ShapeDtypeStruct((b, s, D_MODEL), BF16),
                   head_shape, head_shape,
                   jax.ShapeDtypeStruct((b, nb, 1, D_MODEL), F32)),
        compiler_params=_params(("parallel", "parallel")),
        name="qkv",
    )(h, *args)


def _qkv_decode_body(h_ref, gq_ref, gkv_ref, wq_ref, wk_ref, wv_ref, gqh_ref, gkh_ref, grp_ref,
                     q_ref, k_ref, v_ref):
    n = _rms(h_ref[...])
    hq = (n * gq_ref[...]).astype(BF16)
    c = (n * gkv_ref[...]).astype(BF16)

    def head_norm(y, g_ref):
        pieces = []
        for cb in range(D_MODEL // NORM_COLS):
            ys = y[:, cb * NORM_COLS:(cb + 1) * NORM_COLS]
            ms = _dot((ys * ys).astype(BF16), grp_ref[...])
            pieces.append(ys * lax.rsqrt(ms + EPS))
        return jnp.concatenate(pieces, axis=1) * g_ref[...]

    q_ref[...] = head_norm(_dot(hq, wq_ref[...]), gqh_ref)
    k_ref[...] = head_norm(_dot(c, wk_ref[...]), gkh_ref)
    v_ref[...] = _dot(c, wv_ref[...])


def _qkv_decode(h, g_q_in, g_kv, w_q, w_k, w_v, g_q, g_k):
    m = h.shape[0]
    head = np.arange(NORM_COLS) // HEAD_DIM
    grp = jnp.asarray((head[:, None] == head[None, :]).astype(np.float32) / HEAD_DIM, dtype=BF16)
    args = [g_q_in[None, :], g_kv[None, :], w_q, w_k, w_v,
            jnp.tile(g_q, N_HEADS)[None, :], jnp.tile(g_k, N_HEADS)[None, :], grp]
    specs = ([_const_spec((1, D_MODEL))] * 2 + [_const_spec((D_MODEL, D_MODEL))] * 3
             + [_const_spec((1, D_MODEL))] * 2 + [_const_spec((NORM_COLS, NORM_COLS))])
    row_spec = pl.BlockSpec((m, D_MODEL), lambda i: (0, 0))
    shape = jax.ShapeDtypeStruct((m, D_MODEL), F32)
    return pl.pallas_call(
        _qkv_decode_body,
        grid=(1,),
        in_specs=[row_spec] + specs,
        out_specs=(row_spec, row_spec, row_spec),
        out_shape=(shape, shape, shape),
        compiler_params=_params(("arbitrary",)),
        name="qkv_decode",
    )(h, *args)


def _bucket_of_distance(n):
    n = np.asarray(n, dtype=np.int64)
    max_exact = N_BUCKETS // 2
    nf = np.maximum(n, 1).astype(np.float32)
    large = max_exact + (np.log(nf / np.float32(max_exact)) / np.float32(math.log(MAX_DISTANCE / max_exact))
                         * np.float32(N_BUCKETS - max_exact)).astype(np.int32)
    large = np.minimum(large, N_BUCKETS - 1)
    return np.where(n < max_exact, n, large).astype(np.int32)


def _bias_body(tab_ref, bm_ref, bv_ref, mat_ref, vec_ref):
    h = pl.program_id(0)
    mat = jnp.full(bm_ref.shape, NEG, F32)
    vec = jnp.full(bv_ref.shape, NEG, F32)
    for bkt in range(N_BUCKETS):
        val = tab_ref[bkt, h]
        mat = jnp.where(bm_ref[...] == bkt, val * LOG2E, mat)
        vec = jnp.where(bv_ref[...] == bkt, val, vec)
    mat_ref[0] = mat
    vec_ref[0] = vec


def _bias_tables(rel_table):
    key = np.arange(BLOCK)[:, None]
    qry = np.arange(BLOCK)[None, :]
    own = np.where(qry >= key, _bucket_of_distance(qry - key), -1)
    adj = _bucket_of_distance(BLOCK + qry - key)
    far = np.full((BLOCK, BLOCK), _bucket_of_distance(2 * BLOCK), np.int32)
    bm = jnp.asarray(np.stack([far, adj, own]).astype(np.int32))
    bv = jnp.asarray(_bucket_of_distance(BLOCK - np.arange(BLOCK))[None, :].astype(np.int32))
    return pl.pallas_call(
        _bias_body,
        grid=(N_HEADS,),
        in_specs=[pl.BlockSpec(memory_space=pltpu.SMEM), _const_spec((3, BLOCK, BLOCK)), _const_spec((1, BLOCK))],
        out_specs=(pl.BlockSpec((1, 3, BLOCK, BLOCK), lambda h: (h, 0, 0, 0)),
                   pl.BlockSpec((1, 1, BLOCK), lambda h: (h, 0, 0))),
        out_shape=(jax.ShapeDtypeStruct((N_HEADS, 3, BLOCK, BLOCK), F32),
                   jax.ShapeDtypeStruct((N_HEADS, 1, BLOCK), F32)),
        compiler_params=_params(("arbitrary",)),
        name="rel_bias",
    )(rel_table, bm, bv)


def _top_blocks_penalty(gate, n_valid, own):
    nb = gate.shape[0]
    blk = lax.broadcasted_iota(jnp.int32, gate.shape, 0).astype(F32)
    valid = blk < n_valid
    g = jnp.where(valid, gate, NEG)
    sel = jnp.zeros(gate.shape, jnp.bool_)
    for _ in range(TOPK):
        mx = jnp.max(g, axis=0, keepdims=True)
        first = jnp.min(jnp.where(g == mx, blk, float(nb)), axis=0, keepdims=True)
        pick = blk == first
        sel = jnp.logical_or(sel, pick)
        g = jnp.where(pick, -jnp.inf, g)
    sel = jnp.logical_or(jnp.logical_and(sel, valid), blk == own)
    return jnp.where(sel, 0.0, NEG)


def _moba_body(qt_ref, k_ref, vt_ref, mean_ref, bias_ref, o_ref, qm_ref, pen_ref, sa_ref, sb_ref):
    i = pl.program_id(2)
    heads = range(MOBA_HEADS)
    subs = range(KEY_BLOCKS_PER_STEP)

    def lanes(t):
        lt = t // HEADS_PER_LANE_TILE
        return slice(lt * V7X_LANES, (lt + 1) * V7X_LANES)

    feat = lax.broadcasted_iota(jnp.int32, (V7X_LANES, BLOCK), 0)
    for t in heads:
        half = t % HEADS_PER_LANE_TILE
        in_head = jnp.logical_and(feat >= half * HEAD_DIM, feat < (half + 1) * HEAD_DIM)
        qt = qt_ref[0, lanes(t), :]
        qm_ref[t] = jnp.where(in_head, qt, jnp.zeros_like(qt))

    def rows(j):
        return pl.ds(pl.multiple_of(j * BLOCK, BLOCK), BLOCK)

    def head_rows(t):
        return slice(t * HEAD_DIM, (t + 1) * HEAD_DIM)

    def col_max(x):
        return jnp.max(x, axis=0, keepdims=True)


    def score_near(buf):
        j_adj = jnp.maximum(i - 1, 0)
        no_adj = jnp.where(i >= 1, 0.0, NEG)
        s_adj = [_dot(k_ref[0, rows(j_adj), lanes(t)], qm_ref[t]) for t in heads]
        s_own = [_dot(k_ref[0, rows(i), lanes(t)], qm_ref[t]) for t in heads]
        i_f = i.astype(F32)
        for t in heads:
            mean_hi, mean_lo = _split_bf16(mean_ref[0, :, lanes(t)])
            gate = _dot(mean_hi, qm_ref[t]) + _dot(mean_lo, qm_ref[t])
            pen_ref[t] = _top_blocks_penalty(gate, i_f, i_f)
        meta = []
        for t in heads:
            tiles = [s_adj[t] + bias_ref[t, 1], s_own[t] + bias_ref[t, 2]]
            offs = [pen_ref[t, pl.ds(j_adj, 1), :] + no_adj, jnp.zeros((1, BLOCK), F32)]
            for sub in subs:
                buf[t, sub] = tiles[sub]
            meta.append((offs, jnp.maximum(col_max(tiles[0]) + offs[0], col_max(tiles[1]))))
        return tuple(meta), (j_adj, i)

    n_far = jnp.maximum(i - 1, 0)

    def score_far(c, buf):
        key_rows = pl.ds(pl.multiple_of(c * KEY_ROWS_PER_STEP, KEY_ROWS_PER_STEP), KEY_ROWS_PER_STEP)
        blocks = tuple(c * KEY_BLOCKS_PER_STEP + sub for sub in subs)
        scores = [_dot(k_ref[0, key_rows, lanes(t)], qm_ref[t]) for t in heads]
        meta = []
        for t in heads:
            far_bias = bias_ref[t, 0, 0:1, 0:1]
            tiles = [scores[t][sub * BLOCK:(sub + 1) * BLOCK] for sub in subs]
            offs = [pen_ref[t, pl.ds(blocks[sub], 1), :] + far_bias + jnp.where(blocks[sub] < n_far, 0.0, NEG)
                    for sub in subs]
            cm = col_max(tiles[0]) + offs[0]
            for sub in subs[1:]:
                cm = jnp.maximum(cm, col_max(tiles[sub]) + offs[sub])
            for sub in subs:
                buf[t, sub] = tiles[sub]
            meta.append((offs, cm))
        return tuple(meta), blocks

    def fold(state, buf, meta, blocks):
        new = []
        for t in heads:
            m, l, acc = state[t]
            offs, cm = meta[t]
            m_new = jnp.maximum(m, cm)
            alpha = jnp.exp2(m - m_new)
            l = alpha * l
            acc = alpha * acc
            for sub in subs:
                p = jnp.exp2(buf[t, sub] - (m_new - offs[sub]))
                l = l + jnp.sum(p, axis=0, keepdims=True)
                acc = acc + _dot(vt_ref[0, blocks[sub], head_rows(t), :], p.astype(BF16))
            new.append((m_new, l, acc))
        return tuple(new)

    def step(c2, carry):
        state, meta, blocks = carry
        meta_b, blocks_b = score_far(2 * c2, sb_ref)
        state = fold(state, sa_ref, meta, blocks)
        meta_a, blocks_a = score_far(2 * c2 + 1, sa_ref)
        state = fold(state, sb_ref, meta_b, blocks_b)
        return state, meta_a, blocks_a

    state = tuple((jnp.full((1, BLOCK), -jnp.inf, F32), jnp.zeros((1, BLOCK), F32),
                   jnp.zeros((HEAD_DIM, BLOCK), F32)) for _ in heads)
    n_units = (n_far + KEY_BLOCKS_PER_STEP - 1) // KEY_BLOCKS_PER_STEP
    meta, blocks = score_near(sa_ref)
    state, meta, blocks = lax.fori_loop(0, (n_units + 1) // 2, step, (state, meta, blocks))
    state = fold(state, sa_ref, meta, blocks)
    ot = jnp.concatenate([acc / l for (_, l, acc) in state], axis=0)
    o_ref[0] = ot.T.astype(o_ref.dtype)


def _moba_prompt(qt, kb, vt, means, bias_mat):
    b, s, _ = kb.shape
    nb = s // BLOCK
    assert pl.cdiv(pl.cdiv(max(nb - 2, 0), KEY_BLOCKS_PER_STEP), 2) * 2 * KEY_BLOCKS_PER_STEP <= nb
    width = MOBA_HEADS * HEAD_DIM
    return pl.pallas_call(
        _moba_body,
        grid=(b, D_MODEL // width, nb),
        in_specs=[pl.BlockSpec((1, width, BLOCK), lambda bi, p, i: (bi, p, i)),
                  pl.BlockSpec((1, s, width), lambda bi, p, i: (bi, 0, p)),
                  pl.BlockSpec((1, nb, width, BLOCK), lambda bi, p, i: (bi, 0, p, 0)),
                  pl.BlockSpec((1, nb, width), lambda bi, p, i: (bi, 0, p)),
                  pl.BlockSpec((MOBA_HEADS, 3, BLOCK, BLOCK), lambda bi, p, i: (p, 0, 0, 0))],
        out_specs=pl.BlockSpec((1, BLOCK, width), lambda bi, p, i: (bi, i, p)),
        out_shape=jax.ShapeDtypeStruct((b, s, D_MODEL), BF16),
        scratch_shapes=[pltpu.VMEM((MOBA_HEADS, V7X_LANES, BLOCK), BF16),
                        pltpu.VMEM((MOBA_HEADS, nb, BLOCK), F32)]
        + [pltpu.VMEM((MOBA_HEADS, KEY_BLOCKS_PER_STEP, BLOCK, BLOCK), F32)] * 2,
        compiler_params=_params(("parallel", "parallel", "arbitrary")),
        name="moba_prompt",
    )(qt, kb, vt, means, bias_mat)


def _page_mean_body(pt_ref, *refs):
    del pt_ref
    pages, o_ref = refs[:-1], refs[-1]
    ones = jnp.ones((8, PAGE_SIZE), BF16)
    contract_minor = (((1,), (1,)), ((), ()))
    rows = []
    for t in range(len(pages) // PAGES_PER_BLOCK):
        tot = pages[PAGES_PER_BLOCK * t][0]
        for p in range(1, PAGES_PER_BLOCK):
            tot = tot + pages[PAGES_PER_BLOCK * t + p][0]
        hi, lo = _split_bf16(tot.reshape(D_MODEL, PAGE_SIZE))
        sums = (lax.dot_general(ones, hi, contract_minor, preferred_element_type=F32)
                + lax.dot_general(ones, lo, contract_minor, preferred_element_type=F32))
        rows.append(sums[0:1])
    o_ref[0] = jnp.concatenate(rows, axis=0) * (1.0 / BLOCK)


def _cached_block_means(cache_kt, page_table):
    b, n_pages = page_table.shape
    n_blocks = n_pages // PAGES_PER_BLOCK
    steps = n_blocks // MEAN_BLOCKS_PER_STEP
    pages_per_step = MEAN_BLOCKS_PER_STEP * PAGES_PER_BLOCK

    def page_spec(t):
        return pl.BlockSpec((1, N_HEADS, HEAD_DIM, PAGE_SIZE),
                            lambda bi, g, pt: (pt[bi, g * pages_per_step + t], 0, 0, 0))

    return pl.pallas_call(
        _page_mean_body,
        grid_spec=pltpu.PrefetchScalarGridSpec(
            num_scalar_prefetch=1,
            grid=(b, steps),
            in_specs=[page_spec(t) for t in range(pages_per_step)],
            out_specs=pl.BlockSpec((1, MEAN_BLOCKS_PER_STEP, D_MODEL), lambda bi, g, pt: (bi, g, 0)),
        ),
        out_shape=jax.ShapeDtypeStruct((b, n_blocks, D_MODEL), F32),
        compiler_params=_params(("parallel", "parallel")),
        name="page_means",
    )(page_table, *([cache_kt] * pages_per_step))


def _decode_topk_body(q_ref, cm_ref, seg_ref, idx_ref):
    n_blocks = cm_ref.shape[1]
    hi, lo = _split_bf16(cm_ref[0] * q_ref[0])
    gate = _dot(hi, seg_ref[...]) + _dot(lo, seg_ref[...])
    blk = lax.broadcasted_iota(jnp.int32, gate.shape, 0).astype(F32)
    rows = []
    for _ in range(TOPK):
        mx = jnp.max(gate, axis=0, keepdims=True)
        first = jnp.min(jnp.where(gate == mx, blk, float(n_blocks)), axis=0, keepdims=True)
        rows.append(first)
        gate = jnp.where(blk == first, -jnp.inf, gate)
    idx_ref[0] = jnp.concatenate(rows, axis=0).astype(jnp.int32)


def _decode_topk(q, cmeans):
    b, n_blocks, _ = cmeans.shape
    seg = jnp.asarray((np.arange(D_MODEL)[:, None] // HEAD_DIM == np.arange(N_HEADS)[None, :]).astype(np.float32),
                      dtype=BF16)
    return pl.pallas_call(
        _decode_topk_body,
        grid=(b,),
        in_specs=[pl.BlockSpec((1, 1, D_MODEL), lambda bi: (bi, 0, 0)),
                  pl.BlockSpec((1, n_blocks, D_MODEL), lambda bi: (bi, 0, 0)),
                  _const_spec((D_MODEL, N_HEADS))],
        out_specs=pl.BlockSpec((1, TOPK, N_HEADS), lambda bi: (bi, 0, 0)),
        out_shape=jax.ShapeDtypeStruct((b, TOPK, N_HEADS), jnp.int32),
        compiler_params=_params(("parallel",)),
        name="decode_topk",
    )(q, cmeans, seg)


def _decode_attn_body(pt_ref, idx_ref, tab_ref, q_ref, kn_ref, vn_ref, near_ref, ck_ref, cv_ref, o_ref,
                      kbuf, vbuf, sem, *, n_blocks, n_pages):
    bi = pl.program_id(0)

    def tile(h, r, p):
        return (h * TOPK + r) * PAGES_PER_BLOCK + p

    def gather(row, slot, action):
        def per_head(h, carry):
            for r in range(TOPK):
                blk = idx_ref[row, r * N_HEADS + h]
                for p in range(PAGES_PER_BLOCK):
                    logical = jnp.minimum(blk * PAGES_PER_BLOCK + p, n_pages - 1)
                    phys = pt_ref[row, logical]
                    n = tile(h, r, p)
                    action(pltpu.make_async_copy(ck_ref.at[phys, h], kbuf.at[slot, n], sem.at[0, slot]))
                    action(pltpu.make_async_copy(cv_ref.at[phys, h], vbuf.at[slot, n], sem.at[1, slot]))
            return carry
        lax.fori_loop(0, N_HEADS, per_head, 0)

    slot = bi % 2

    @pl.when(bi == 0)
    def _():
        gather(bi, slot, lambda c: c.start())

    @pl.when(bi + 1 < pl.num_programs(0))
    def _():
        gather(bi + 1, 1 - slot, lambda c: c.start())

    gather(bi, slot, lambda c: c.wait())

    heads = range(N_HEADS)
    sel = [(r, p) for r in range(TOPK) for p in range(PAGES_PER_BLOCK)]
    q = [q_ref[0, :, h:h + 1] for h in heads]
    s_own = [jnp.sum(q[h] * kn_ref[0, :, h:h + 1], axis=0, keepdims=True) * SCALE + tab_ref[0, h] for h in heads]
    scores = []
    for h in heads:
        far = tab_ref[N_BUCKETS - 1, h]
        row = []
        for r, p in sel:
            blk = idx_ref[bi, r * N_HEADS + h]
            s = jnp.sum(kbuf[slot, tile(h, r, p)] * q[h], axis=0, keepdims=True) * SCALE
            bias = jnp.where(blk == n_blocks - 1, near_ref[h, :, p * PAGE_SIZE:(p + 1) * PAGE_SIZE], far)
            row.append(jnp.where(blk < n_blocks, s + bias, NEG))
        scores.append(row)
    m = []
    for h in heads:
        mh = scores[h][0]
        for s in scores[h][1:]:
            mh = jnp.maximum(mh, s)
        m.append(jnp.maximum(s_own[h], jnp.max(mh, axis=1, keepdims=True)))
    probs = [[jnp.exp(s - m[h]) for s in scores[h]] for h in heads]
    for h in heads:
        p_own = jnp.exp(s_own[h] - m[h])
        ps = probs[h][0]
        for p in probs[h][1:]:
            ps = ps + p
        l = p_own + jnp.sum(ps, axis=1, keepdims=True)
        acc = vbuf[slot, tile(h, 0, 0)] * probs[h][0]
        for n in range(1, len(sel)):
            acc = acc + vbuf[slot, tile(h, *sel[n])] * probs[h][n]
        o_ref[0, :, h:h + 1] = (p_own * vn_ref[0, :, h:h + 1] + jnp.sum(acc, axis=1, keepdims=True)) / l


def _decode_attn(q, k_new, v_new, cache_kt, cache_vt, page_table, idx, rel_table, near_bias):
    b, n_pages = page_table.shape
    n_blocks = n_pages // PAGES_PER_BLOCK
    n_tiles = N_HEADS * TOPK * PAGES_PER_BLOCK
    tok_spec = pl.BlockSpec((1, HEAD_DIM, N_HEADS), lambda bi, pt, ix: (bi, 0, 0))
    return pl.pallas_call(
        functools.partial(_decode_attn_body, n_blocks=n_blocks, n_pages=n_pages),
        grid_spec=pltpu.PrefetchScalarGridSpec(
            num_scalar_prefetch=2,
            grid=(b,),
            in_specs=[pl.BlockSpec(memory_space=pltpu.SMEM), tok_spec, tok_spec, tok_spec,
                      pl.BlockSpec((N_HEADS, 1, BLOCK), lambda bi, pt, ix: (0, 0, 0)),
                      pl.BlockSpec(memory_space=pl.ANY), pl.BlockSpec(memory_space=pl.ANY)],
            out_specs=tok_spec,
            scratch_shapes=[pltpu.VMEM((2, n_tiles, HEAD_DIM, PAGE_SIZE), F32),
                            pltpu.VMEM((2, n_tiles, HEAD_DIM, PAGE_SIZE), F32),
                            pltpu.SemaphoreType.DMA((2, 2))],
        ),
        out_shape=jax.ShapeDtypeStruct(q.shape, F32),
        compiler_params=_params(("arbitrary",)),
        name="decode_attn",
    )(page_table, idx, rel_table, q, k_new, v_new, near_bias, cache_kt, cache_vt)


def _pick_tile(m, target):
    return target if m % target == 0 else m


def kernel(x_prompt, x_sample, cache_k, cache_v, page_table, rel_table, g_mix, g_ffn, w_gin, g_gv, w_sp, b_sp,
           w_gout, g_kv, w_k, w_v, g_k, w_q, g_q, w_o, w_f1, w_f3, w_f2):
    bsz, seq, _ = x_prompt.shape
    dec = x_sample.shape[0]
    assert page_table.shape[1] % PAGES_PER_BLOCK == 0 and seq % BLOCK == 0
    cast = lambda w: w.astype(BF16)
    w_gin_b, w_gout_b = cast(w_gin[0]), cast(w_gout[0])
    w_q_b, w_k_b, w_v_b, w_o_b = cast(w_q[0]), cast(w_k), cast(w_v), cast(w_o[0])
    w_f1_b, w_f3_b, w_f2_b = cast(w_f1), cast(w_f3), cast(w_f2)
    qkv_w = (g_mix[1], g_kv, w_q_b, w_k_b, w_v_b, g_q[0], g_k)

    bias_mat, bias_near = _bias_tables(rel_table)

    xp = x_prompt.reshape(bsz * seq, D_MODEL)
    tm = _pick_tile(bsz * seq, 512)
    h = _gmlp(xp, g_mix[0], w_gin_b, g_gv[0], w_sp[0], b_sp[0], w_gout_b, single_token=False, tm=tm)
    h = _ffn(h, g_ffn[0], w_f1_b[0], w_f3_b[0], w_f2_b[0], tm=tm)
    qt, vt, kb, prompt_kt, prompt_vt, means = _qkv_prompt(h.reshape(bsz, seq, D_MODEL), *qkv_w)
    attn = _moba_prompt(qt, kb, vt, means.reshape(bsz, seq // BLOCK, D_MODEL), bias_mat)
    y_prompt = _ffn(h, g_ffn[1], w_f1_b[1], w_f3_b[1], w_f2_b[1], tm=tm,
                    attn=attn.reshape(bsz * seq, D_MODEL), w_o=w_o_b).reshape(bsz, seq, D_MODEL)

    cache_kt = cache_k.transpose(0, 1, 3, 2)
    cache_vt = cache_v.transpose(0, 1, 3, 2)
    xs = x_sample.reshape(dec, D_MODEL)
    hs, v_rows = _gmlp(xs, g_mix[0], w_gin_b, g_gv[0], w_sp[0], b_sp[0], w_gout_b, single_token=True, tm=dec)
    hs = _ffn(hs, g_ffn[0], w_f1_b[0], w_f3_b[0], w_f2_b[0], tm=dec)
    qs, ks, vs = _qkv_decode(hs, *qkv_w)
    cmeans = _cached_block_means(cache_kt, page_table)
    idx = _decode_topk(qs.reshape(dec, 1, D_MODEL), cmeans).reshape(dec, TOPK * N_HEADS)
    as_cols = lambda a: a.reshape(dec, N_HEADS, HEAD_DIM).transpose(0, 2, 1)
    attn_s = _decode_attn(as_cols(qs), as_cols(ks), as_cols(vs), cache_kt, cache_vt, page_table, idx,
                          rel_table, bias_near)
    attn_s = attn_s.transpose(0, 2, 1).reshape(dec, D_MODEL).astype(BF16)
    y_sample = _ffn(hs, g_ffn[1], w_f1_b[1], w_f3_b[1], w_f2_b[1], tm=dec,
                    attn=attn_s, w_o=w_o_b).reshape(dec, 1, D_MODEL)

    as_heads = lambda a: a.reshape(dec, N_HEADS, 1, HEAD_DIM)
    return (y_prompt, y_sample, prompt_kt.transpose(0, 1, 3, 2), prompt_vt.transpose(0, 1, 3, 2),
            as_heads(ks), as_heads(vs), v_rows.reshape(1, dec, 1, D_GATE))
```

```python
import functools
import math

import numpy as np
import jax
import jax.numpy as jnp
from jax import lax
from jax.experimental import pallas as pl
from jax.experimental.pallas import tpu as pltpu

F32 = jnp.float32
BF16 = jnp.bfloat16

D_MODEL = 1024
N_HEADS = 16
HEAD_DIM = 64
CHUNK = 128
N_GROUPS = 8
GROUP_DIM = 128
D_GATE = 1024
BLOCK = 256
TOPK = 3
N_BUCKETS = 32
MAX_DISTANCE = 128
PAGE_SIZE = 128
PAGES_PER_BLOCK = BLOCK // PAGE_SIZE
EPS = 1e-6
NEG = -1e30
FAR_BUCKET = N_BUCKETS - 1
SCALE = HEAD_DIM ** -0.5

V7X_LANES = 128
V7X_MXU_DIM = 256
V7X_VMEM_LIMIT = 56 * 1024 * 1024

LOG2E = math.log2(math.e)
FFN_CHUNK = V7X_MXU_DIM
HEADS_PER_LANE_TILE = V7X_LANES // HEAD_DIM
NORM_COLS = V7X_MXU_DIM
MEAN_BLOCKS_PER_STEP = 8
KEY_BLOCKS_PER_STEP = 2
KEY_ROWS_PER_STEP = KEY_BLOCKS_PER_STEP * BLOCK
MOBA_HEADS = 4
PV_ROWS = HEAD_DIM + 16


def _rms(x):
    return x * lax.rsqrt(jnp.mean(x * x, axis=-1, keepdims=True) + EPS)


def _gelu_tanh(x):
    c = math.sqrt(2.0 / math.pi)
    return x * (0.5 * (1.0 + jnp.tanh(c * (x + 0.044715 * (x * x * x)))))


def _dot(a, b):
    return jnp.dot(a, b, preferred_element_type=F32)


def _split_bf16(x):
    hi = x.astype(BF16)
    return hi, (x - hi.astype(F32)).astype(BF16)


def _params(sem, vmem=V7X_VMEM_LIMIT):
    return pltpu.CompilerParams(dimension_semantics=sem, vmem_limit_bytes=vmem)


def _const_spec(shape):
    zeros = (0,) * len(shape)
    return pl.BlockSpec(shape, lambda *_: zeros, pipeline_mode=pl.Buffered(1))


def _gmlp_body(x_ref, gmix_ref, win_ref, ggv_ref, wsp_ref, bsp_ref, wout_ref, *rest, single_token):
    x = x_ref[...]
    hn = (_rms(x) * gmix_ref[...]).astype(BF16)
    uv = _gelu_tanh(_dot(hn, win_ref[...]))
    u = uv[:, :D_GATE]
    v = _rms(uv[:, D_GATE:]) * ggv_ref[...]
    if single_token:
        h_ref, v_ref = rest
        v_ref[...] = v
        mixed = v * wsp_ref[...] + bsp_ref[...]
    else:
        h_ref, mixed_ref = rest
        vb = v.astype(BF16)
        row = lax.broadcasted_iota(jnp.int32, (CHUNK, CHUNK), 0)
        col = lax.broadcasted_iota(jnp.int32, (CHUNK, CHUNK), 1)
        for g in range(N_GROUPS):
            cols = slice(g * GROUP_DIM, (g + 1) * GROUP_DIM)
            w = jnp.where(row >= col, wsp_ref[g], 0.0).astype(BF16)
            for n in range(x.shape[0] // CHUNK):
                rows = slice(n * CHUNK, (n + 1) * CHUNK)
                mixed_ref[rows, cols] = _dot(w, vb[rows, cols]) + bsp_ref[:, cols]
        mixed = mixed_ref[...]
    h_ref[...] = x + _dot((u * mixed).astype(BF16), wout_ref[...])


def _gmlp(x, g_mix, w_in, g_gv, w_sp, b_sp, w_out, *, single_token, tm):
    m = x.shape[0]
    row_spec = pl.BlockSpec((tm, D_MODEL), lambda i: (i, 0))
    if single_token:
        wsp_arg = jnp.repeat(w_sp[:, 0, 0], GROUP_DIM)[None, :]
        bsp_arg = jnp.repeat(b_sp[:, 0], GROUP_DIM)[None, :]
        wsp_spec = _const_spec((1, D_GATE))
        bsp_spec = _const_spec((1, D_GATE))
        out_shape = (jax.ShapeDtypeStruct((m, D_MODEL), F32), jax.ShapeDtypeStruct((m, D_GATE), F32))
        out_specs = (row_spec, pl.BlockSpec((tm, D_GATE), lambda i: (i, 0)))
        scratch = []
    else:
        wsp_arg = w_sp
        bsp_arg = jnp.repeat(b_sp.T, GROUP_DIM, axis=1)
        wsp_spec = _const_spec((N_GROUPS, CHUNK, CHUNK))
        bsp_spec = _const_spec((CHUNK, D_GATE))
        out_shape = jax.ShapeDtypeStruct((m, D_MODEL), F32)
        out_specs = row_spec
        scratch = [pltpu.VMEM((tm, D_GATE), F32)]
    return pl.pallas_call(
        functools.partial(_gmlp_body, single_token=single_token),
        grid=(m // tm,),
        in_specs=[row_spec, _const_spec((1, D_MODEL)), _const_spec((D_MODEL, 2 * D_GATE)),
                  _const_spec((1, D_GATE)), wsp_spec, bsp_spec, _const_spec((D_GATE, D_MODEL))],
        out_specs=out_specs,
        out_shape=out_shape,
        scratch_shapes=scratch,
        compiler_params=_params(("parallel",)),
        name="gmlp_decode" if single_token else "gmlp",
    )(x, g_mix[None, :], w_in, g_gv[None, :], wsp_arg, bsp_arg, w_out)


def _ffn_body(*refs, has_proj):
    if has_proj:
        h_ref, a_ref, wo_ref, g_ref, w1_ref, w3_ref, w2_ref, o_ref = refs
        h = h_ref[...] + _dot(a_ref[...], wo_ref[...])
    else:
        h_ref, g_ref, w1_ref, w3_ref, w2_ref, o_ref = refs
        h = h_ref[...]
    n = (_rms(h) * g_ref[...]).astype(BF16)
    acc = h
    for c in range(w1_ref.shape[1] // FFN_CHUNK):
        cols = slice(c * FFN_CHUNK, (c + 1) * FFN_CHUNK)
        a = _dot(n, w1_ref[:, cols])
        b = _dot(n, w3_ref[:, cols])
        acc = acc + _dot((a * jax.nn.sigmoid(a) * b).astype(BF16), w2_ref[cols, :])
    o_ref[...] = acc


def _ffn(h, g, w1, w3, w2, *, tm, attn=None, w_o=None):
    m = h.shape[0]
    d_ff = w1.shape[1]
    row_spec = pl.BlockSpec((tm, D_MODEL), lambda i: (i, 0))
    has_proj = attn is not None
    args = [h]
    specs = [row_spec]
    if has_proj:
        args += [attn, w_o]
        specs += [row_spec, _const_spec((D_MODEL, D_MODEL))]
    args += [g[None, :], w1, w3, w2]
    specs += [_const_spec((1, D_MODEL)), _const_spec((D_MODEL, d_ff)), _const_spec((D_MODEL, d_ff)),
              _const_spec((d_ff, D_MODEL))]
    return pl.pallas_call(
        functools.partial(_ffn_body, has_proj=has_proj),
        grid=(m // tm,),
        in_specs=specs,
        out_specs=row_spec,
        out_shape=jax.ShapeDtypeStruct((m, D_MODEL), F32),
        compiler_params=_params(("parallel",)),
        name="proj_ffn" if has_proj else "ffn",
    )(*args)


def _qkv_prompt_body(h_ref, gq_ref, gkv_ref, wq_ref, wk_ref, wv_ref, gqh_ref, gkh_ref,
                     qt_ref, vt_ref, kb_ref, k_ref, v_ref, mean_ref):
    n_t = _rms(h_ref[0]).T
    tok = n_t.shape[1]
    hq = (n_t * gq_ref[...]).astype(BF16)
    c = (n_t * gkv_ref[...]).astype(BF16)

    def head_norm(y, g_ref):
        y3 = y.reshape(N_HEADS, HEAD_DIM, tok)
        ms = jnp.mean(y3 * y3, axis=1, keepdims=True)
        return y3 * lax.rsqrt(ms + EPS) * g_ref[...]

    q3 = head_norm(_dot(wq_ref[...], hq), gqh_ref)
    k3 = head_norm(_dot(wk_ref[...], c), gkh_ref)
    v = _dot(wv_ref[...], c)
    qt_ref[0] = (q3 * (SCALE * LOG2E)).reshape(D_MODEL, tok).astype(BF16)
    v3 = v.reshape(N_HEADS, HEAD_DIM, tok)
    ones = jnp.ones((N_HEADS, PV_ROWS - HEAD_DIM, tok), F32)
    vt_ref[0, 0] = jnp.concatenate([v3, ones], axis=1).reshape(N_HEADS * PV_ROWS, tok).astype(BF16)
    k_ref[0] = k3
    v_ref[0] = v3
    k_tok = k3.reshape(D_MODEL, tok).T
    kb_ref[0] = k_tok.astype(BF16)
    mean_ref[0, 0] = jnp.mean(k_tok, axis=0, keepdims=True)


def _qkv_prompt(h, g_q_in, g_kv, w_q, w_k, w_v, g_q, g_k):
    b, s, _ = h.shape
    nb = s // BLOCK
    col = lambda g: g[:, None]
    args = [col(g_q_in), col(g_kv), w_q, w_k, w_v, col(g_q), col(g_k)]
    specs = ([_const_spec((D_MODEL, 1))] * 2 + [_const_spec((D_MODEL, D_MODEL))] * 3
             + [_const_spec((HEAD_DIM, 1))] * 2)
    head_spec = pl.BlockSpec((1, N_HEADS, HEAD_DIM, BLOCK), lambda bi, i: (bi, 0, 0, i))
    head_shape = jax.ShapeDtypeStruct((b, N_HEADS, HEAD_DIM, s), F32)
    return pl.pallas_call(
        _qkv_prompt_body,
        grid=(b, nb),
        in_specs=[pl.BlockSpec((1, BLOCK, D_MODEL), lambda bi, i: (bi, i, 0))] + specs,
        out_specs=(pl.BlockSpec((1, D_MODEL, BLOCK), lambda bi, i: (bi, 0, i)),
                   pl.BlockSpec((1, 1, N_HEADS * PV_ROWS, BLOCK), lambda bi, i: (bi, i, 0, 0)),
                   pl.BlockSpec((1, BLOCK, D_MODEL), lambda bi, i: (bi, i, 0)),
                   head_spec, head_spec,
                   pl.BlockSpec((1, 1, 1, D_MODEL), lambda bi, i: (bi, i, 0, 0))),
        out_shape=(jax.ShapeDtypeStruct((b, D_MODEL, s), BF16),
                   jax.ShapeDtypeStruct((b, nb, N_HEADS * PV_ROWS, BLOCK), BF16),
                   jax.ShapeDtypeStruct((b, s, D_MODEL), BF16),
                   head_shape, head_shape,
                   jax.ShapeDtypeStruct((b, nb, 1, D_MODEL), F32)),
        compiler_params=_params(("parallel", "parallel")),
        name="qkv",
    )(h, *args)


def _qkv_decode_body(h_ref, gq_ref, gkv_ref, wq_ref, wk_ref, wv_ref, gqh_ref, gkh_ref, grp_ref,
                     q_ref, k_ref, v_ref):
    n = _rms(h_ref[...])
    hq = (n * gq_ref[...]).astype(BF16)
    c = (n * gkv_ref[...]).astype(BF16)

    def head_norm(y, g_ref):
        pieces = []
        for cb in range(D_MODEL // NORM_COLS):
            ys = y[:, cb * NORM_COLS:(cb + 1) * NORM_COLS]
            ms = _dot((ys * ys).astype(BF16), grp_ref[...])
            pieces.append(ys * lax.rsqrt(ms + EPS))
        return jnp.concatenate(pieces, axis=1) * g_ref[...]

    def project(x, w_ref):
        return lax.dot_general(x, w_ref[...], (((1,), (1,)), ((), ())), preferred_element_type=F32)

    q_ref[...] = head_norm(project(hq, wq_ref), gqh_ref)
    k_ref[...] = head_norm(project(c, wk_ref), gkh_ref)
    v_ref[...] = project(c, wv_ref)


def _qkv_decode(h, g_q_in, g_kv, w_q, w_k, w_v, g_q, g_k):
    m = h.shape[0]
    head = np.arange(NORM_COLS) // HEAD_DIM
    grp = jnp.asarray((head[:, None] == head[None, :]).astype(np.float32) / HEAD_DIM, dtype=BF16)
    args = [g_q_in[None, :], g_kv[None, :], w_q, w_k, w_v,
            jnp.tile(g_q, N_HEADS)[None, :], jnp.tile(g_k, N_HEADS)[None, :], grp]
    specs = ([_const_spec((1, D_MODEL))] * 2 + [_const_spec((D_MODEL, D_MODEL))] * 3
             + [_const_spec((1, D_MODEL))] * 2 + [_const_spec((NORM_COLS, NORM_COLS))])
    row_spec = pl.BlockSpec((m, D_MODEL), lambda i: (0, 0))
    shape = jax.ShapeDtypeStruct((m, D_MODEL), F32)
    return pl.pallas_call(
        _qkv_decode_body,
        grid=(1,),
        in_specs=[row_spec] + specs,
        out_specs=(row_spec, row_spec, row_spec),
        out_shape=(shape, shape, shape),
        compiler_params=_params(("arbitrary",)),
        name="qkv_decode",
    )(h, *args)


def _bucket_of_distance(n):
    n = np.asarray(n, dtype=np.int64)
    max_exact = N_BUCKETS // 2
    nf = np.maximum(n, 1).astype(np.float32)
    large = max_exact + (np.log(nf / np.float32(max_exact)) / np.float32(math.log(MAX_DISTANCE / max_exact))
                         * np.float32(N_BUCKETS - max_exact)).astype(np.int32)
    large = np.minimum(large, N_BUCKETS - 1)
    return np.where(n < max_exact, n, large).astype(np.int32)


def _bias_body(tab_ref, bm_ref, bv_ref, mat_ref, vec_ref):
    h = pl.program_id(0)
    mat = jnp.full(bm_ref.shape, NEG, F32)
    vec = jnp.full(bv_ref.shape, NEG, F32)
    for bkt in range(N_BUCKETS):
        val = tab_ref[bkt, h]
        mat = jnp.where(bm_ref[...] == bkt, val * LOG2E, mat)
        vec = jnp.where(bv_ref[...] == bkt, val, vec)
    mat_ref[0, 0] = jnp.full(bm_ref.shape[1:], tab_ref[FAR_BUCKET, h] * LOG2E, F32)
    mat_ref[0, 1:] = mat
    vec_ref[0] = vec


def _bias_tables(rel_table):
    key = np.arange(BLOCK)[:, None]
    qry = np.arange(BLOCK)[None, :]
    own = np.where(qry >= key, _bucket_of_distance(qry - key), -1)
    adj = _bucket_of_distance(BLOCK + qry - key)
    assert _bucket_of_distance(BLOCK + 1) == FAR_BUCKET
    bm = jnp.asarray(np.stack([adj, own]).astype(np.int32))
    bv = jnp.asarray(_bucket_of_distance(BLOCK - np.arange(BLOCK))[None, :].astype(np.int32))
    return pl.pallas_call(
        _bias_body,
        grid=(N_HEADS,),
        in_specs=[pl.BlockSpec(memory_space=pltpu.SMEM), _const_spec((2, BLOCK, BLOCK)), _const_spec((1, BLOCK))],
        out_specs=(pl.BlockSpec((1, 3, BLOCK, BLOCK), lambda h: (h, 0, 0, 0)),
                   pl.BlockSpec((1, 1, BLOCK), lambda h: (h, 0, 0))),
        out_shape=(jax.ShapeDtypeStruct((N_HEADS, 3, BLOCK, BLOCK), F32),
                   jax.ShapeDtypeStruct((N_HEADS, 1, BLOCK), F32)),
        compiler_params=_params(("arbitrary",)),
        name="rel_bias",
    )(rel_table, bm, bv)


def _top_blocks_penalty(gate, n_valid, own):
    nb = gate.shape[0]
    blk = lax.broadcasted_iota(jnp.int32, gate.shape, 0).astype(F32)
    valid = blk < n_valid
    g = jnp.where(valid, gate, NEG)
    sel = jnp.zeros(gate.shape, jnp.bool_)
    for _ in range(TOPK):
        mx = jnp.max(g, axis=0, keepdims=True)
        first = jnp.min(jnp.where(g == mx, blk, float(nb)), axis=0, keepdims=True)
        pick = blk == first
        sel = jnp.logical_or(sel, pick)
        g = jnp.where(pick, -jnp.inf, g)
    sel = jnp.logical_or(jnp.logical_and(sel, valid), blk == own)
    return jnp.where(sel, 0.0, NEG)


def _moba_body(qt_ref, k_ref, vt_ref, mean_ref, bias_ref, o_ref, qm_ref, pen_ref, sa_ref, sb_ref):
    i = pl.program_id(2)
    heads = range(MOBA_HEADS)
    subs = range(KEY_BLOCKS_PER_STEP)

    def lanes(t):
        lt = t // HEADS_PER_LANE_TILE
        return slice(lt * V7X_LANES, (lt + 1) * V7X_LANES)

    feat = lax.broadcasted_iota(jnp.int32, (V7X_LANES, BLOCK), 0)
    for t in heads:
        half = t % HEADS_PER_LANE_TILE
        in_head = jnp.logical_and(feat >= half * HEAD_DIM, feat < (half + 1) * HEAD_DIM)
        qt = qt_ref[0, lanes(t), :]
        qm_ref[t] = jnp.where(in_head, qt, jnp.zeros_like(qt))

    def rows(j):
        return pl.ds(pl.multiple_of(j * BLOCK, BLOCK), BLOCK)

    def head_rows(t):
        return slice(t * PV_ROWS, (t + 1) * PV_ROWS)

    def col_max(x):
        return jnp.max(x, axis=0, keepdims=True)


    def score_near(buf):
        j_adj = jnp.maximum(i - 1, 0)
        no_adj = jnp.where(i >= 1, 0.0, NEG)
        s_adj = [_dot(k_ref[0, rows(j_adj), lanes(t)], qm_ref[t]) for t in heads]
        s_own = [_dot(k_ref[0, rows(i), lanes(t)], qm_ref[t]) for t in heads]
        i_f = i.astype(F32)
        for t in heads:
            mean_hi, mean_lo = _split_bf16(mean_ref[0, :, lanes(t)])
            gate = _dot(mean_hi, qm_ref[t]) + _dot(mean_lo, qm_ref[t])
            pen_ref[t] = _top_blocks_penalty(gate, i_f, i_f)
        meta = []
        for t in heads:
            tiles = [s_adj[t] + bias_ref[t, 1], s_own[t] + bias_ref[t, 2]]
            offs = [pen_ref[t, pl.ds(j_adj, 1), :] + no_adj, jnp.zeros((1, BLOCK), F32)]
            for sub in subs:
                buf[t, sub] = tiles[sub]
            meta.append((offs, jnp.maximum(col_max(tiles[0]) + offs[0], col_max(tiles[1]))))
        return tuple(meta), (j_adj, i)

    n_far = jnp.maximum(i - 1, 0)

    def score_far(c, buf):
        key_rows = pl.ds(pl.multiple_of(c * KEY_ROWS_PER_STEP, KEY_ROWS_PER_STEP), KEY_ROWS_PER_STEP)
        blocks = tuple(c * KEY_BLOCKS_PER_STEP + sub for sub in subs)
        scores = [_dot(k_ref[0, key_rows, lanes(t)], qm_ref[t]) for t in heads]
        meta = []
        for t in heads:
            far_bias = bias_ref[t, 0, 0:1, 0:1]
            tiles = [scores[t][sub * BLOCK:(sub + 1) * BLOCK] for sub in subs]
            offs = [pen_ref[t, pl.ds(blocks[sub], 1), :] + far_bias + jnp.where(blocks[sub] < n_far, 0.0, NEG)
                    for sub in subs]
            cm = col_max(tiles[0]) + offs[0]
            for sub in subs[1:]:
                cm = jnp.maximum(cm, col_max(tiles[sub]) + offs[sub])
            for sub in subs:
                buf[t, sub] = tiles[sub]
            meta.append((offs, cm))
        return tuple(meta), blocks

    def fold(state, buf, meta, blocks):
        new = []
        for t in heads:
            m, acc = state[t]
            offs, cm = meta[t]
            m_new = jnp.maximum(m, cm)
            acc = jnp.exp2(m - m_new) * acc
            for sub in subs:
                p = jnp.exp2(buf[t, sub] - (m_new - offs[sub]))
                acc = acc + _dot(vt_ref[0, blocks[sub], head_rows(t), :], p.astype(BF16))
            new.append((m_new, acc))
        return tuple(new)

    def step(c2, carry):
        state, meta, blocks = carry
        meta_b, blocks_b = score_far(2 * c2, sb_ref)
        state = fold(state, sa_ref, meta, blocks)
        meta_a, blocks_a = score_far(2 * c2 + 1, sa_ref)
        state = fold(state, sb_ref, meta_b, blocks_b)
        return state, meta_a, blocks_a

    state = tuple((jnp.full((1, BLOCK), -jnp.inf, F32), jnp.zeros((PV_ROWS, BLOCK), F32)) for _ in heads)
    n_units = (n_far + KEY_BLOCKS_PER_STEP - 1) // KEY_BLOCKS_PER_STEP
    meta, blocks = score_near(sa_ref)
    state, meta, blocks = lax.fori_loop(0, (n_units + 1) // 2, step, (state, meta, blocks))
    state = fold(state, sa_ref, meta, blocks)
    ot = jnp.concatenate([acc[:HEAD_DIM] / acc[HEAD_DIM:HEAD_DIM + 1] for (_, acc) in state], axis=0)
    o_ref[0] = ot.T.astype(o_ref.dtype)


def _moba_prompt(qt, kb, vt, means, bias_mat):
    b, s, _ = kb.shape
    nb = s // BLOCK
    assert pl.cdiv(pl.cdiv(max(nb - 2, 0), KEY_BLOCKS_PER_STEP), 2) * 2 * KEY_BLOCKS_PER_STEP <= nb
    width = MOBA_HEADS * HEAD_DIM
    return pl.pallas_call(
        _moba_body,
        grid=(b, D_MODEL // width, nb),
        in_specs=[pl.BlockSpec((1, width, BLOCK), lambda bi, p, i: (bi, p, i)),
                  pl.BlockSpec((1, s, width), lambda bi, p, i: (bi, 0, p)),
                  pl.BlockSpec((1, nb, MOBA_HEADS * PV_ROWS, BLOCK), lambda bi, p, i: (bi, 0, p, 0)),
                  pl.BlockSpec((1, nb, width), lambda bi, p, i: (bi, 0, p)),
                  pl.BlockSpec((MOBA_HEADS, 3, BLOCK, BLOCK), lambda bi, p, i: (p, 0, 0, 0))],
        out_specs=pl.BlockSpec((1, BLOCK, width), lambda bi, p, i: (bi, i, p)),
        out_shape=jax.ShapeDtypeStruct((b, s, D_MODEL), BF16),
        scratch_shapes=[pltpu.VMEM((MOBA_HEADS, V7X_LANES, BLOCK), BF16),
                        pltpu.VMEM((MOBA_HEADS, nb, BLOCK), F32)]
        + [pltpu.VMEM((MOBA_HEADS, KEY_BLOCKS_PER_STEP, BLOCK, BLOCK), F32)] * 2,
        compiler_params=_params(("parallel", "parallel", "arbitrary")),
        name="moba_prompt",
    )(qt, kb, vt, means, bias_mat)


def _page_mean_body(pt_ref, *refs):
    del pt_ref
    pages, o_ref = refs[:-1], refs[-1]
    ones = jnp.ones((8, PAGE_SIZE), BF16)
    contract_minor = (((1,), (1,)), ((), ()))
    rows = []
    for t in range(len(pages) // PAGES_PER_BLOCK):
        tot = pages[PAGES_PER_BLOCK * t][0]
        for p in range(1, PAGES_PER_BLOCK):
            tot = tot + pages[PAGES_PER_BLOCK * t + p][0]
        hi, lo = _split_bf16(tot.reshape(D_MODEL, PAGE_SIZE))
        sums = (lax.dot_general(ones, hi, contract_minor, preferred_element_type=F32)
                + lax.dot_general(ones, lo, contract_minor, preferred_element_type=F32))
        rows.append(sums[0:1])
    o_ref[0] = jnp.concatenate(rows, axis=0) * (1.0 / BLOCK)


def _cached_block_means(cache_kt, page_table):
    b, n_pages = page_table.shape
    n_blocks = n_pages // PAGES_PER_BLOCK
    steps = n_blocks // MEAN_BLOCKS_PER_STEP
    pages_per_step = MEAN_BLOCKS_PER_STEP * PAGES_PER_BLOCK

    def page_spec(t):
        return pl.BlockSpec((1, N_HEADS, HEAD_DIM, PAGE_SIZE),
                            lambda bi, g, pt: (pt[bi, g * pages_per_step + t], 0, 0, 0))

    return pl.pallas_call(
        _page_mean_body,
        grid_spec=pltpu.PrefetchScalarGridSpec(
            num_scalar_prefetch=1,
            grid=(b, steps),
            in_specs=[page_spec(t) for t in range(pages_per_step)],
            out_specs=pl.BlockSpec((1, MEAN_BLOCKS_PER_STEP, D_MODEL), lambda bi, g, pt: (bi, g, 0)),
        ),
        out_shape=jax.ShapeDtypeStruct((b, n_blocks, D_MODEL), F32),
        compiler_params=_params(("parallel", "parallel")),
        name="page_means",
    )(page_table, *([cache_kt] * pages_per_step))


def _decode_topk_body(q_ref, cm_ref, seg_ref, idx_ref):
    n_blocks = cm_ref.shape[1]
    hi, lo = _split_bf16(cm_ref[0] * q_ref[0])
    gate = _dot(hi, seg_ref[...]) + _dot(lo, seg_ref[...])
    blk = lax.broadcasted_iota(jnp.int32, gate.shape, 0).astype(F32)
    rows = []
    for _ in range(TOPK):
        mx = jnp.max(gate, axis=0, keepdims=True)
        first = jnp.min(jnp.where(gate == mx, blk, float(n_blocks)), axis=0, keepdims=True)
        rows.append(first)
        gate = jnp.where(blk == first, -jnp.inf, gate)
    idx_ref[0] = jnp.concatenate(rows, axis=0).astype(jnp.int32)


def _decode_topk(q, cmeans):
    b, n_blocks, _ = cmeans.shape
    seg = jnp.asarray((np.arange(D_MODEL)[:, None] // HEAD_DIM == np.arange(N_HEADS)[None, :]).astype(np.float32),
                      dtype=BF16)
    return pl.pallas_call(
        _decode_topk_body,
        grid=(b,),
        in_specs=[pl.BlockSpec((1, 1, D_MODEL), lambda bi: (bi, 0, 0)),
                  pl.BlockSpec((1, n_blocks, D_MODEL), lambda bi: (bi, 0, 0)),
                  _const_spec((D_MODEL, N_HEADS))],
        out_specs=pl.BlockSpec((1, TOPK, N_HEADS), lambda bi: (bi, 0, 0)),
        out_shape=jax.ShapeDtypeStruct((b, TOPK, N_HEADS), jnp.int32),
        compiler_params=_params(("parallel",)),
        name="decode_topk",
    )(q, cmeans, seg)


def _decode_attn_body(pt_ref, idx_ref, tab_ref, q_ref, kn_ref, vn_ref, near_ref, ck_ref, cv_ref, o_ref,
                      kbuf, vbuf, sem, *, n_blocks, n_pages):
    bi = pl.program_id(0)

    def tile(h, r, p):
        return (h * TOPK + r) * PAGES_PER_BLOCK + p

    def gather(row, slot, action):
        def per_head(h, carry):
            for r in range(TOPK):
                blk = idx_ref[row, r * N_HEADS + h]
                for p in range(PAGES_PER_BLOCK):
                    logical = jnp.minimum(blk * PAGES_PER_BLOCK + p, n_pages - 1)
                    phys = pt_ref[row, logical]
                    n = tile(h, r, p)
                    action(pltpu.make_async_copy(ck_ref.at[phys, h], kbuf.at[slot, n], sem.at[0, slot]))
                    action(pltpu.make_async_copy(cv_ref.at[phys, h], vbuf.at[slot, n], sem.at[1, slot]))
            return carry
        lax.fori_loop(0, N_HEADS, per_head, 0)

    slot = bi % 2

    @pl.when(bi == 0)
    def _():
        gather(bi, slot, lambda c: c.start())

    @pl.when(bi + 1 < pl.num_programs(0))
    def _():
        gather(bi + 1, 1 - slot, lambda c: c.start())

    gather(bi, slot, lambda c: c.wait())

    heads = range(N_HEADS)
    sel = [(r, p) for r in range(TOPK) for p in range(PAGES_PER_BLOCK)]
    q = [q_ref[0, :, h:h + 1] for h in heads]
    s_own = [jnp.sum(q[h] * kn_ref[0, :, h:h + 1], axis=0, keepdims=True) * SCALE + tab_ref[0, h] for h in heads]
    scores = []
    for h in heads:
        far = tab_ref[N_BUCKETS - 1, h]
        row = []
        for r, p in sel:
            blk = idx_ref[bi, r * N_HEADS + h]
            s = jnp.sum(kbuf[slot, tile(h, r, p)] * q[h], axis=0, keepdims=True) * SCALE
            bias = jnp.where(blk == n_blocks - 1, near_ref[h, :, p * PAGE_SIZE:(p + 1) * PAGE_SIZE], far)
            row.append(jnp.where(blk < n_blocks, s + bias, NEG))
        scores.append(row)
    m = []
    for h in heads:
        mh = scores[h][0]
        for s in scores[h][1:]:
            mh = jnp.maximum(mh, s)
        m.append(jnp.maximum(s_own[h], jnp.max(mh, axis=1, keepdims=True)))
    probs = [[jnp.exp(s - m[h]) for s in scores[h]] for h in heads]
    for h in heads:
        p_own = jnp.exp(s_own[h] - m[h])
        ps = probs[h][0]
        for p in probs[h][1:]:
            ps = ps + p
        l = p_own + jnp.sum(ps, axis=1, keepdims=True)
        acc = vbuf[slot, tile(h, 0, 0)] * probs[h][0]
        for n in range(1, len(sel)):
            acc = acc + vbuf[slot, tile(h, *sel[n])] * probs[h][n]
        o_ref[0, :, h:h + 1] = (p_own * vn_ref[0, :, h:h + 1] + jnp.sum(acc, axis=1, keepdims=True)) / l


def _decode_attn(q, k_new, v_new, cache_kt, cache_vt, page_table, idx, rel_table, near_bias):
    b, n_pages = page_table.shape
    n_blocks = n_pages // PAGES_PER_BLOCK
    n_tiles = N_HEADS * TOPK * PAGES_PER_BLOCK
    tok_spec = pl.BlockSpec((1, HEAD_DIM, N_HEADS), lambda bi, pt, ix: (bi, 0, 0))
    return pl.pallas_call(
        functools.partial(_decode_attn_body, n_blocks=n_blocks, n_pages=n_pages),
        grid_spec=pltpu.PrefetchScalarGridSpec(
            num_scalar_prefetch=2,
            grid=(b,),
            in_specs=[pl.BlockSpec(memory_space=pltpu.SMEM), tok_spec, tok_spec, tok_spec,
                      pl.BlockSpec((N_HEADS, 1, BLOCK), lambda bi, pt, ix: (0, 0, 0)),
                      pl.BlockSpec(memory_space=pl.ANY), pl.BlockSpec(memory_space=pl.ANY)],
            out_specs=tok_spec,
            scratch_shapes=[pltpu.VMEM((2, n_tiles, HEAD_DIM, PAGE_SIZE), F32),
                            pltpu.VMEM((2, n_tiles, HEAD_DIM, PAGE_SIZE), F32),
                            pltpu.SemaphoreType.DMA((2, 2))],
        ),
        out_shape=jax.ShapeDtypeStruct(q.shape, F32),
        compiler_params=_params(("arbitrary",)),
        name="decode_attn",
    )(page_table, idx, rel_table, q, k_new, v_new, near_bias, cache_kt, cache_vt)


def _pick_tile(m, target):
    return target if m % target == 0 else m


def kernel(x_prompt, x_sample, cache_k, cache_v, page_table, rel_table, g_mix, g_ffn, w_gin, g_gv, w_sp, b_sp,
           w_gout, g_kv, w_k, w_v, g_k, w_q, g_q, w_o, w_f1, w_f3, w_f2):
    bsz, seq, _ = x_prompt.shape
    dec = x_sample.shape[0]
    assert page_table.shape[1] % PAGES_PER_BLOCK == 0 and seq % BLOCK == 0
    cast = lambda w: w.astype(BF16)
    layers = range(w_f1.shape[0])
    w_gin_b, w_gout_b, w_o_b = cast(w_gin[0]), cast(w_gout[0]), cast(w_o[0])
    w_f1_b, w_f3_b, w_f2_b = ([cast(w[l]) for l in layers] for w in (w_f1, w_f3, w_f2))
    qkv_w = (g_mix[1], g_kv, cast(w_q[0].T), cast(w_k.T), cast(w_v.T), g_q[0], g_k)

    bias_mat, bias_near = _bias_tables(rel_table)

    xp = x_prompt.reshape(bsz * seq, D_MODEL)
    tm = _pick_tile(bsz * seq, 512)
    h = _gmlp(xp, g_mix[0], w_gin_b, g_gv[0], w_sp[0], b_sp[0], w_gout_b, single_token=False, tm=tm)
    h = _ffn(h, g_ffn[0], w_f1_b[0], w_f3_b[0], w_f2_b[0], tm=tm)
    qt, vt, kb, prompt_kt, prompt_vt, means = _qkv_prompt(h.reshape(bsz, seq, D_MODEL), *qkv_w)
    attn = _moba_prompt(qt, kb, vt, means.reshape(bsz, seq // BLOCK, D_MODEL), bias_mat)
    y_prompt = _ffn(h, g_ffn[1], w_f1_b[1], w_f3_b[1], w_f2_b[1], tm=tm,
                    attn=attn.reshape(bsz * seq, D_MODEL), w_o=w_o_b).reshape(bsz, seq, D_MODEL)

    cache_kt = cache_k.transpose(0, 1, 3, 2)
    cache_vt = cache_v.transpose(0, 1, 3, 2)
    xs = x_sample.reshape(dec, D_MODEL)
    hs, v_rows = _gmlp(xs, g_mix[0], w_gin_b, g_gv[0], w_sp[0], b_sp[0], w_gout_b, single_token=True, tm=dec)
    hs = _ffn(hs, g_ffn[0], w_f1_b[0], w_f3_b[0], w_f2_b[0], tm=dec)
    qs, ks, vs = _qkv_decode(hs, *qkv_w)
    cmeans = _cached_block_means(cache_kt, page_table)
    idx = _decode_topk(qs.reshape(dec, 1, D_MODEL), cmeans).reshape(dec, TOPK * N_HEADS)
    as_cols = lambda a: a.reshape(dec, N_HEADS, HEAD_DIM).transpose(0, 2, 1)
    attn_s = _decode_attn(as_cols(qs), as_cols(ks), as_cols(vs), cache_kt, cache_vt, page_table, idx,
                          rel_table, bias_near)
    attn_s = attn_s.transpose(0, 2, 1).reshape(dec, D_MODEL).astype(BF16)
    y_sample = _ffn(hs, g_ffn[1], w_f1_b[1], w_f3_b[1], w_f2_b[1], tm=dec,
                    attn=attn_s, w_o=w_o_b).reshape(dec, 1, D_MODEL)

    as_heads = lambda a: a.reshape(dec, N_HEADS, 1, HEAD_DIM)
    return (y_prompt, y_sample, prompt_kt.transpose(0, 1, 3, 2), prompt_vt.transpose(0, 1, 3, 2),
            as_heads(ks), as_heads(vs), v_rows.reshape(1, dec, 1, D_GATE))
```

```python
import functools
import math

import numpy as np
import jax
import jax.numpy as jnp
from jax import lax
from jax.experimental import pallas as pl
from jax.experimental.pallas import tpu as pltpu

F32 = jnp.float32
BF16 = jnp.bfloat16

D_MODEL = 1024
N_HEADS = 16
HEAD_DIM = 64
CHUNK = 128
N_GROUPS = 8
GROUP_DIM = 128
D_GATE = 1024
BLOCK = 256
TOPK = 3
N_BUCKETS = 32
MAX_DISTANCE = 128
PAGE_SIZE = 128
PAGES_PER_BLOCK = BLOCK // PAGE_SIZE
EPS = 1e-6
NEG = -1e30
FAR_BUCKET = N_BUCKETS - 1
SCALE = HEAD_DIM ** -0.5

V7X_LANES = 128
V7X_MXU_DIM = 256
V7X_VMEM_LIMIT = 56 * 1024 * 1024

LOG2E = math.log2(math.e)
FFN_CHUNK = V7X_MXU_DIM
HEADS_PER_LANE_TILE = V7X_LANES // HEAD_DIM
NORM_COLS = V7X_MXU_DIM
MEAN_BLOCKS_PER_STEP = 16
TOPK_ROWS_PER_STEP = 8
KEY_BLOCKS_PER_STEP = 2
KEY_ROWS_PER_STEP = KEY_BLOCKS_PER_STEP * BLOCK
MOBA_HEADS = 8
PV_ROWS = HEAD_DIM + 16


def _rms(x):
    return x * lax.rsqrt(jnp.mean(x * x, axis=-1, keepdims=True) + EPS)


def _gelu_tanh(x):
    c = math.sqrt(2.0 / math.pi)
    return x * (0.5 * (1.0 + jnp.tanh(c * (x + 0.044715 * (x * x * x)))))


def _dot(a, b):
    return jnp.dot(a, b, preferred_element_type=F32)


def _split_bf16(x):
    hi = x.astype(BF16)
    return hi, (x - hi.astype(F32)).astype(BF16)


def _params(sem, vmem=V7X_VMEM_LIMIT):
    return pltpu.CompilerParams(dimension_semantics=sem, vmem_limit_bytes=vmem)


def _const_spec(shape):
    zeros = (0,) * len(shape)
    return pl.BlockSpec(shape, lambda *_: zeros, pipeline_mode=pl.Buffered(1))


def _gmlp_body(x_ref, gmix_ref, win_ref, ggv_ref, wsp_ref, bsp_ref, wout_ref, *rest, single_token):
    x = x_ref[...]
    hn = (_rms(x) * gmix_ref[...]).astype(BF16)
    uv = _gelu_tanh(_dot(hn, win_ref[...]))
    u = uv[:, :D_GATE]
    v = _rms(uv[:, D_GATE:]) * ggv_ref[...]
    if single_token:
        h_ref, v_ref = rest
        v_ref[...] = v
        mixed = v * wsp_ref[...] + bsp_ref[...]
    else:
        h_ref, mixed_ref = rest
        vb = v.astype(BF16)
        row = lax.broadcasted_iota(jnp.int32, (CHUNK, CHUNK), 0)
        col = lax.broadcasted_iota(jnp.int32, (CHUNK, CHUNK), 1)
        for g in range(N_GROUPS):
            cols = slice(g * GROUP_DIM, (g + 1) * GROUP_DIM)
            w = jnp.where(row >= col, wsp_ref[g], 0.0).astype(BF16)
            for n in range(x.shape[0] // CHUNK):
                rows = slice(n * CHUNK, (n + 1) * CHUNK)
                mixed_ref[rows, cols] = _dot(w, vb[rows, cols]) + bsp_ref[:, cols]
        mixed = mixed_ref[...]
    h_ref[...] = x + _dot((u * mixed).astype(BF16), wout_ref[...])


def _gmlp(x, g_mix, w_in, g_gv, w_sp, b_sp, w_out, *, single_token, tm):
    m = x.shape[0]
    row_spec = pl.BlockSpec((tm, D_MODEL), lambda i: (i, 0))
    if single_token:
        wsp_arg = jnp.repeat(w_sp[:, 0, 0], GROUP_DIM)[None, :]
        bsp_arg = jnp.repeat(b_sp[:, 0], GROUP_DIM)[None, :]
        wsp_spec = _const_spec((1, D_GATE))
        bsp_spec = _const_spec((1, D_GATE))
        out_shape = (jax.ShapeDtypeStruct((m, D_MODEL), F32), jax.ShapeDtypeStruct((m, D_GATE), F32))
        out_specs = (row_spec, pl.BlockSpec((tm, D_GATE), lambda i: (i, 0)))
        scratch = []
    else:
        wsp_arg = w_sp
        bsp_arg = jnp.repeat(b_sp.T, GROUP_DIM, axis=1)
        wsp_spec = _const_spec((N_GROUPS, CHUNK, CHUNK))
        bsp_spec = _const_spec((CHUNK, D_GATE))
        out_shape = jax.ShapeDtypeStruct((m, D_MODEL), F32)
        out_specs = row_spec
        scratch = [pltpu.VMEM((tm, D_GATE), F32)]
    return pl.pallas_call(
        functools.partial(_gmlp_body, single_token=single_token),
        grid=(m // tm,),
        in_specs=[row_spec, _const_spec((1, D_MODEL)), _const_spec((D_MODEL, 2 * D_GATE)),
                  _const_spec((1, D_GATE)), wsp_spec, bsp_spec, _const_spec((D_GATE, D_MODEL))],
        out_specs=out_specs,
        out_shape=out_shape,
        scratch_shapes=scratch,
        compiler_params=_params(("parallel",)),
        name="gmlp_decode" if single_token else "gmlp",
    )(x, g_mix[None, :], w_in, g_gv[None, :], wsp_arg, bsp_arg, w_out)


def _ffn_body(*refs, has_proj):
    if has_proj:
        h_ref, a_ref, wo_ref, g_ref, w1_ref, w3_ref, w2_ref, o_ref = refs
        h = h_ref[...] + _dot(a_ref[...], wo_ref[...])
    else:
        h_ref, g_ref, w1_ref, w3_ref, w2_ref, o_ref = refs
        h = h_ref[...]
    n = (_rms(h) * g_ref[...]).astype(BF16)
    acc = h
    for c in range(w1_ref.shape[1] // FFN_CHUNK):
        cols = slice(c * FFN_CHUNK, (c + 1) * FFN_CHUNK)
        a = _dot(n, w1_ref[:, cols])
        b = _dot(n, w3_ref[:, cols])
        acc = acc + _dot((a * jax.nn.sigmoid(a) * b).astype(BF16), w2_ref[cols, :])
    o_ref[...] = acc


def _ffn(h, g, w1, w3, w2, layer, *, tm, attn=None, w_o=None):
    m = h.shape[0]
    d_ff = w1.shape[2]

    def layer_spec(rows, cols):
        return pl.BlockSpec((None, rows, cols), lambda *_: (layer, 0, 0), pipeline_mode=pl.Buffered(1))

    row_spec = pl.BlockSpec((tm, D_MODEL), lambda i: (i, 0))
    has_proj = attn is not None
    args = [h]
    specs = [row_spec]
    if has_proj:
        args += [attn, w_o]
        specs += [row_spec, _const_spec((D_MODEL, D_MODEL))]
    args += [g[None, :], w1, w3, w2]
    specs += [_const_spec((1, D_MODEL)), layer_spec(D_MODEL, d_ff), layer_spec(D_MODEL, d_ff),
              layer_spec(d_ff, D_MODEL)]
    return pl.pallas_call(
        functools.partial(_ffn_body, has_proj=has_proj),
        grid=(m // tm,),
        in_specs=specs,
        out_specs=row_spec,
        out_shape=jax.ShapeDtypeStruct((m, D_MODEL), F32),
        compiler_params=_params(("parallel",)),
        name="proj_ffn" if has_proj else "ffn",
    )(*args)


def _qkv_prompt_body(h_ref, gq_ref, gkv_ref, wq_ref, wk_ref, wv_ref, gqh_ref, gkh_ref,
                     qt_ref, vt_ref, kb_ref, k_ref, v_ref, mean_ref):
    n_t = _rms(h_ref[0]).T
    tok = n_t.shape[1]
    hq = (n_t * gq_ref[...]).astype(BF16)
    c = (n_t * gkv_ref[...]).astype(BF16)

    def head_norm(y, g_ref):
        y3 = y.reshape(N_HEADS, HEAD_DIM, tok)
        ms = jnp.mean(y3 * y3, axis=1, keepdims=True)
        return y3 * lax.rsqrt(ms + EPS) * g_ref[...]

    q3 = head_norm(_dot(wq_ref[...], hq), gqh_ref)
    k3 = head_norm(_dot(wk_ref[...], c), gkh_ref)
    v = _dot(wv_ref[...], c)
    qt_ref[0] = (q3 * (SCALE * LOG2E)).reshape(D_MODEL, tok).astype(BF16)
    v3 = v.reshape(N_HEADS, HEAD_DIM, tok)
    ones = jnp.ones((N_HEADS, PV_ROWS - HEAD_DIM, tok), F32)
    vt_ref[0, 0] = jnp.concatenate([v3, ones], axis=1).reshape(N_HEADS * PV_ROWS, tok).astype(BF16)
    k_ref[0] = k3
    v_ref[0] = v3
    k_tok = k3.reshape(D_MODEL, tok).T
    kb_ref[0] = k_tok.astype(BF16)
    mean_ref[0, 0] = jnp.mean(k_tok, axis=0, keepdims=True)


def _qkv_prompt(h, g_q_in, g_kv, w_q, w_k, w_v, g_q, g_k):
    b, s, _ = h.shape
    nb = s // BLOCK
    col = lambda g: g[:, None]
    args = [col(g_q_in), col(g_kv), w_q, w_k, w_v, col(g_q), col(g_k)]
    specs = ([_const_spec((D_MODEL, 1))] * 2 + [_const_spec((D_MODEL, D_MODEL))] * 3
             + [_const_spec((HEAD_DIM, 1))] * 2)
    head_spec = pl.BlockSpec((1, N_HEADS, HEAD_DIM, BLOCK), lambda bi, i: (bi, 0, 0, i))
    head_shape = jax.ShapeDtypeStruct((b, N_HEADS, HEAD_DIM, s), F32)
    return pl.pallas_call(
        _qkv_prompt_body,
        grid=(b, nb),
        in_specs=[pl.BlockSpec((1, BLOCK, D_MODEL), lambda bi, i: (bi, i, 0))] + specs,
        out_specs=(pl.BlockSpec((1, D_MODEL, BLOCK), lambda bi, i: (bi, 0, i)),
                   pl.BlockSpec((1, 1, N_HEADS * PV_ROWS, BLOCK), lambda bi, i: (bi, i, 0, 0)),
                   pl.BlockSpec((1, BLOCK, D_MODEL), lambda bi, i: (bi, i, 0)),
                   head_spec, head_spec,
                   pl.BlockSpec((1, 1, 1, D_MODEL), lambda bi, i: (bi, i, 0, 0))),
        out_shape=(jax.ShapeDtypeStruct((b, D_MODEL, s), BF16),
                   jax.ShapeDtypeStruct((b, nb, N_HEADS * PV_ROWS, BLOCK), BF16),
                   jax.ShapeDtypeStruct((b, s, D_MODEL), BF16),
                   head_shape, head_shape,
                   jax.ShapeDtypeStruct((b, nb, 1, D_MODEL), F32)),
        compiler_params=_params(("parallel", "parallel")),
        name="qkv",
    )(h, *args)


def _qkv_decode_body(h_ref, gq_ref, gkv_ref, wq_ref, wk_ref, wv_ref, gqh_ref, gkh_ref, grp_ref,
                     q_ref, k_ref, v_ref):
    n = _rms(h_ref[...])
    hq = (n * gq_ref[...]).astype(BF16)
    c = (n * gkv_ref[...]).astype(BF16)

    def head_norm(y, g_ref):
        pieces = []
        for cb in range(D_MODEL // NORM_COLS):
            ys = y[:, cb * NORM_COLS:(cb + 1) * NORM_COLS]
            ms = _dot((ys * ys).astype(BF16), grp_ref[...])
            pieces.append(ys * lax.rsqrt(ms + EPS))
        return jnp.concatenate(pieces, axis=1) * g_ref[...]

    def project(x, w_ref):
        return lax.dot_general(x, w_ref[...], (((1,), (1,)), ((), ())), preferred_element_type=F32)

    q_ref[...] = head_norm(project(hq, wq_ref), gqh_ref)
    k_ref[...] = head_norm(project(c, wk_ref), gkh_ref)
    v_ref[...] = project(c, wv_ref)


def _qkv_decode(h, g_q_in, g_kv, w_q, w_k, w_v, g_q, g_k):
    m = h.shape[0]
    head = np.arange(NORM_COLS) // HEAD_DIM
    grp = jnp.asarray((head[:, None] == head[None, :]).astype(np.float32) / HEAD_DIM, dtype=BF16)
    args = [g_q_in[None, :], g_kv[None, :], w_q, w_k, w_v,
            jnp.tile(g_q, N_HEADS)[None, :], jnp.tile(g_k, N_HEADS)[None, :], grp]
    specs = ([_const_spec((1, D_MODEL))] * 2 + [_const_spec((D_MODEL, D_MODEL))] * 3
             + [_const_spec((1, D_MODEL))] * 2 + [_const_spec((NORM_COLS, NORM_COLS))])
    row_spec = pl.BlockSpec((m, D_MODEL), lambda i: (0, 0))
    shape = jax.ShapeDtypeStruct((m, D_MODEL), F32)
    return pl.pallas_call(
        _qkv_decode_body,
        grid=(1,),
        in_specs=[row_spec] + specs,
        out_specs=(row_spec, row_spec, row_spec),
        out_shape=(shape, shape, shape),
        compiler_params=_params(("arbitrary",)),
        name="qkv_decode",
    )(h, *args)


def _bucket_of_distance(n):
    n = np.asarray(n, dtype=np.int64)
    max_exact = N_BUCKETS // 2
    nf = np.maximum(n, 1).astype(np.float32)
    large = max_exact + (np.log(nf / np.float32(max_exact)) / np.float32(math.log(MAX_DISTANCE / max_exact))
                         * np.float32(N_BUCKETS - max_exact)).astype(np.int32)
    large = np.minimum(large, N_BUCKETS - 1)
    return np.where(n < max_exact, n, large).astype(np.int32)


def _bias_body(tab_ref, bm_ref, bv_ref, mat_ref, vec_ref):
    h = pl.program_id(0)
    mat = jnp.full(bm_ref.shape, NEG, F32)
    vec = jnp.full(bv_ref.shape, NEG, F32)
    for bkt in range(N_BUCKETS):
        val = tab_ref[bkt, h]
        mat = jnp.where(bm_ref[...] == bkt, val * LOG2E, mat)
        vec = jnp.where(bv_ref[...] == bkt, val, vec)
    mat_ref[0, 0] = jnp.full(bm_ref.shape[1:], tab_ref[FAR_BUCKET, h] * LOG2E, F32)
    mat_ref[0, 1:] = mat
    vec_ref[0] = vec


def _bias_tables(rel_table):
    key = np.arange(BLOCK)[:, None]
    qry = np.arange(BLOCK)[None, :]
    own = np.where(qry >= key, _bucket_of_distance(qry - key), -1)
    adj = _bucket_of_distance(BLOCK + qry - key)
    assert _bucket_of_distance(BLOCK + 1) == FAR_BUCKET
    bm = jnp.asarray(np.stack([adj, own]).astype(np.int32))
    bv = jnp.asarray(_bucket_of_distance(BLOCK - np.arange(BLOCK))[None, :].astype(np.int32))
    return pl.pallas_call(
        _bias_body,
        grid=(N_HEADS,),
        in_specs=[pl.BlockSpec(memory_space=pltpu.SMEM), _const_spec((2, BLOCK, BLOCK)), _const_spec((1, BLOCK))],
        out_specs=(pl.BlockSpec((1, 3, BLOCK, BLOCK), lambda h: (h, 0, 0, 0)),
                   pl.BlockSpec((1, 1, BLOCK), lambda h: (h, 0, 0))),
        out_shape=(jax.ShapeDtypeStruct((N_HEADS, 3, BLOCK, BLOCK), F32),
                   jax.ShapeDtypeStruct((N_HEADS, 1, BLOCK), F32)),
        compiler_params=_params(("arbitrary",)),
        name="rel_bias",
    )(rel_table, bm, bv)


def _top_blocks_penalty(gate, n_valid, own):
    nb = gate.shape[0]
    blk = lax.broadcasted_iota(jnp.int32, gate.shape, 0).astype(F32)
    valid = blk < n_valid
    g = jnp.where(valid, gate, NEG)
    sel = jnp.zeros(gate.shape, jnp.bool_)
    for _ in range(TOPK):
        mx = jnp.max(g, axis=0, keepdims=True)
        first = jnp.min(jnp.where(g == mx, blk, float(nb)), axis=0, keepdims=True)
        pick = blk == first
        sel = jnp.logical_or(sel, pick)
        g = jnp.where(pick, -jnp.inf, g)
    sel = jnp.logical_or(jnp.logical_and(sel, valid), blk == own)
    return jnp.where(sel, 0.0, NEG)


def _moba_body(qt_ref, k_ref, vt_ref, mean_ref, bias_ref, o_ref, qm_ref, pen_ref, sa_ref, sb_ref):
    i = pl.program_id(2)
    heads = range(MOBA_HEADS)
    subs = range(KEY_BLOCKS_PER_STEP)

    def lanes(t):
        lt = t // HEADS_PER_LANE_TILE
        return slice(lt * V7X_LANES, (lt + 1) * V7X_LANES)

    feat = lax.broadcasted_iota(jnp.int32, (V7X_LANES, BLOCK), 0)
    for t in heads:
        half = t % HEADS_PER_LANE_TILE
        in_head = jnp.logical_and(feat >= half * HEAD_DIM, feat < (half + 1) * HEAD_DIM)
        qt = qt_ref[0, lanes(t), :]
        qm_ref[t] = jnp.where(in_head, qt, jnp.zeros_like(qt))

    def rows(j):
        return pl.ds(pl.multiple_of(j * BLOCK, BLOCK), BLOCK)

    def head_rows(t):
        return slice(t * PV_ROWS, (t + 1) * PV_ROWS)

    def col_max(x):
        return jnp.max(x, axis=0, keepdims=True)


    def score_near(buf):
        j_adj = jnp.maximum(i - 1, 0)
        no_adj = jnp.where(i >= 1, 0.0, NEG)
        s_adj = [_dot(k_ref[0, rows(j_adj), lanes(t)], qm_ref[t]) for t in heads]
        s_own = [_dot(k_ref[0, rows(i), lanes(t)], qm_ref[t]) for t in heads]
        i_f = i.astype(F32)
        for t in heads:
            mean_hi, mean_lo = _split_bf16(mean_ref[0, :, lanes(t)])
            gate = _dot(mean_hi, qm_ref[t]) + _dot(mean_lo, qm_ref[t])
            pen_ref[t] = _top_blocks_penalty(gate, i_f, i_f)
        meta = []
        for t in heads:
            tiles = [s_adj[t] + bias_ref[t, 1], s_own[t] + bias_ref[t, 2]]
            offs = [pen_ref[t, pl.ds(j_adj, 1), :] + no_adj, jnp.zeros((1, BLOCK), F32)]
            for sub in subs:
                buf[t, sub] = tiles[sub]
            meta.append((offs, jnp.maximum(col_max(tiles[0]) + offs[0], col_max(tiles[1]))))
        return tuple(meta), (j_adj, i)

    n_far = jnp.maximum(i - 1, 0)

    def score_far(c, buf):
        key_rows = pl.ds(pl.multiple_of(c * KEY_ROWS_PER_STEP, KEY_ROWS_PER_STEP), KEY_ROWS_PER_STEP)
        blocks = tuple(c * KEY_BLOCKS_PER_STEP + sub for sub in subs)
        scores = [_dot(k_ref[0, key_rows, lanes(t)], qm_ref[t]) for t in heads]
        meta = []
        for t in heads:
            far_bias = bias_ref[t, 0, 0:1, 0:1]
            tiles = [scores[t][sub * BLOCK:(sub + 1) * BLOCK] for sub in subs]
            offs = [pen_ref[t, pl.ds(blocks[sub], 1), :] + far_bias + jnp.where(blocks[sub] < n_far, 0.0, NEG)
                    for sub in subs]
            cm = col_max(tiles[0]) + offs[0]
            for sub in subs[1:]:
                cm = jnp.maximum(cm, col_max(tiles[sub]) + offs[sub])
            for sub in subs:
                buf[t, sub] = tiles[sub]
            meta.append((offs, cm))
        return tuple(meta), blocks

    def fold(state, buf, meta, blocks):
        new = []
        for t in heads:
            m, acc = state[t]
            offs, cm = meta[t]
            m_new = jnp.maximum(m, cm)
            acc = jnp.exp2(m - m_new) * acc
            for sub in subs:
                p = jnp.exp2(buf[t, sub] - (m_new - offs[sub]))
                acc = acc + _dot(vt_ref[0, blocks[sub], head_rows(t), :], p.astype(BF16))
            new.append((m_new, acc))
        return tuple(new)

    def step(c2, carry):
        state, meta, blocks = carry
        meta_b, blocks_b = score_far(2 * c2, sb_ref)
        state = fold(state, sa_ref, meta, blocks)
        meta_a, blocks_a = score_far(2 * c2 + 1, sa_ref)
        state = fold(state, sb_ref, meta_b, blocks_b)
        return state, meta_a, blocks_a

    state = tuple((jnp.full((1, BLOCK), -jnp.inf, F32), jnp.zeros((PV_ROWS, BLOCK), F32)) for _ in heads)
    n_units = (n_far + KEY_BLOCKS_PER_STEP - 1) // KEY_BLOCKS_PER_STEP
    meta, blocks = score_near(sa_ref)
    state, meta, blocks = lax.fori_loop(0, (n_units + 1) // 2, step, (state, meta, blocks))
    state = fold(state, sa_ref, meta, blocks)
    ot = jnp.concatenate([acc[:HEAD_DIM] / acc[HEAD_DIM:HEAD_DIM + 1] for (_, acc) in state], axis=0)
    o_ref[0] = ot.T.astype(o_ref.dtype)


def _moba_prompt(qt, kb, vt, means, bias_mat):
    b, s, _ = kb.shape
    nb = s // BLOCK
    assert pl.cdiv(pl.cdiv(max(nb - 2, 0), KEY_BLOCKS_PER_STEP), 2) * 2 * KEY_BLOCKS_PER_STEP <= nb
    width = MOBA_HEADS * HEAD_DIM
    return pl.pallas_call(
        _moba_body,
        grid=(b, D_MODEL // width, nb),
        in_specs=[pl.BlockSpec((1, width, BLOCK), lambda bi, p, i: (bi, p, i)),
                  pl.BlockSpec((1, s, width), lambda bi, p, i: (bi, 0, p)),
                  pl.BlockSpec((1, nb, MOBA_HEADS * PV_ROWS, BLOCK), lambda bi, p, i: (bi, 0, p, 0)),
                  pl.BlockSpec((1, nb, width), lambda bi, p, i: (bi, 0, p)),
                  pl.BlockSpec((MOBA_HEADS, 3, BLOCK, BLOCK), lambda bi, p, i: (p, 0, 0, 0))],
        out_specs=pl.BlockSpec((1, BLOCK, width), lambda bi, p, i: (bi, i, p)),
        out_shape=jax.ShapeDtypeStruct((b, s, D_MODEL), BF16),
        scratch_shapes=[pltpu.VMEM((MOBA_HEADS, V7X_LANES, BLOCK), BF16),
                        pltpu.VMEM((MOBA_HEADS, nb, BLOCK), F32)]
        + [pltpu.VMEM((MOBA_HEADS, KEY_BLOCKS_PER_STEP, BLOCK, BLOCK), F32)] * 2,
        compiler_params=_params(("parallel", "parallel", "arbitrary")),
        name="moba_prompt",
    )(qt, kb, vt, means, bias_mat)


def _page_mean_body(pt_ref, *refs):
    del pt_ref
    pages, o_ref = refs[:-1], refs[-1]
    ones = jnp.ones((8, PAGE_SIZE), BF16)
    contract_minor = (((1,), (1,)), ((), ()))
    rows = []
    for t in range(len(pages) // PAGES_PER_BLOCK):
        tot = pages[PAGES_PER_BLOCK * t][0]
        for p in range(1, PAGES_PER_BLOCK):
            tot = tot + pages[PAGES_PER_BLOCK * t + p][0]
        hi, lo = _split_bf16(tot.reshape(D_MODEL, PAGE_SIZE))
        sums = (lax.dot_general(ones, hi, contract_minor, preferred_element_type=F32)
                + lax.dot_general(ones, lo, contract_minor, preferred_element_type=F32))
        rows.append(sums[0:1])
    o_ref[0] = jnp.concatenate(rows, axis=0) * (1.0 / BLOCK)


def _cached_block_means(cache_kt, page_table):
    b, n_pages = page_table.shape
    n_blocks = n_pages // PAGES_PER_BLOCK
    steps = n_blocks // MEAN_BLOCKS_PER_STEP
    pages_per_step = MEAN_BLOCKS_PER_STEP * PAGES_PER_BLOCK

    def page_spec(t):
        return pl.BlockSpec((1, N_HEADS, HEAD_DIM, PAGE_SIZE),
                            lambda bi, g, pt: (pt[bi, g * pages_per_step + t], 0, 0, 0))

    return pl.pallas_call(
        _page_mean_body,
        grid_spec=pltpu.PrefetchScalarGridSpec(
            num_scalar_prefetch=1,
            grid=(b, steps),
            in_specs=[page_spec(t) for t in range(pages_per_step)],
            out_specs=pl.BlockSpec((1, MEAN_BLOCKS_PER_STEP, D_MODEL), lambda bi, g, pt: (bi, g, 0)),
        ),
        out_shape=jax.ShapeDtypeStruct((b, n_blocks, D_MODEL), F32),
        compiler_params=_params(("parallel", "parallel")),
        name="page_means",
    )(page_table, *([cache_kt] * pages_per_step))


def _decode_topk_body(q_ref, cm_ref, seg_ref, idx_ref):
    n_blocks = cm_ref.shape[1]
    blk = lax.broadcasted_iota(jnp.int32, (n_blocks, N_HEADS), 0).astype(F32)
    for r in range(cm_ref.shape[0]):
        hi, lo = _split_bf16(cm_ref[r] * q_ref[r])
        gate = _dot(hi, seg_ref[...]) + _dot(lo, seg_ref[...])
        rows = []
        for _ in range(TOPK):
            mx = jnp.max(gate, axis=0, keepdims=True)
            first = jnp.min(jnp.where(gate == mx, blk, float(n_blocks)), axis=0, keepdims=True)
            rows.append(first)
            gate = jnp.where(blk == first, -jnp.inf, gate)
        idx_ref[r] = jnp.concatenate(rows, axis=0).astype(jnp.int32)


def _decode_topk(q, cmeans):
    b, n_blocks, _ = cmeans.shape
    rows = _pick_tile(b, TOPK_ROWS_PER_STEP)
    seg = jnp.asarray((np.arange(D_MODEL)[:, None] // HEAD_DIM == np.arange(N_HEADS)[None, :]).astype(np.float32),
                      dtype=BF16)
    return pl.pallas_call(
        _decode_topk_body,
        grid=(b // rows,),
        in_specs=[pl.BlockSpec((rows, 1, D_MODEL), lambda bi: (bi, 0, 0)),
                  pl.BlockSpec((rows, n_blocks, D_MODEL), lambda bi: (bi, 0, 0)),
                  _const_spec((D_MODEL, N_HEADS))],
        out_specs=pl.BlockSpec((rows, TOPK, N_HEADS), lambda bi: (bi, 0, 0)),
        out_shape=jax.ShapeDtypeStruct((b, TOPK, N_HEADS), jnp.int32),
        compiler_params=_params(("parallel",)),
        name="decode_topk",
    )(q, cmeans, seg)


def _decode_attn_body(pt_ref, idx_ref, tab_ref, q_ref, kn_ref, vn_ref, near_ref, ck_ref, cv_ref, o_ref,
                      kbuf, vbuf, sem, *, n_blocks, n_pages):
    bi = pl.program_id(0)

    def tile(h, r, p):
        return (h * TOPK + r) * PAGES_PER_BLOCK + p

    def gather(row, slot, action):
        def per_head(h, carry):
            for r in range(TOPK):
                blk = idx_ref[row, r * N_HEADS + h]
                for p in range(PAGES_PER_BLOCK):
                    logical = jnp.minimum(blk * PAGES_PER_BLOCK + p, n_pages - 1)
                    phys = pt_ref[row, logical]
                    n = tile(h, r, p)
                    action(pltpu.make_async_copy(ck_ref.at[phys, h], kbuf.at[slot, n], sem.at[0, slot]))
                    action(pltpu.make_async_copy(cv_ref.at[phys, h], vbuf.at[slot, n], sem.at[1, slot]))
            return carry
        lax.fori_loop(0, N_HEADS, per_head, 0)

    slot = bi % 2

    @pl.when(bi == 0)
    def _():
        gather(bi, slot, lambda c: c.start())

    @pl.when(bi + 1 < pl.num_programs(0))
    def _():
        gather(bi + 1, 1 - slot, lambda c: c.start())

    gather(bi, slot, lambda c: c.wait())

    heads = range(N_HEADS)
    sel = [(r, p) for r in range(TOPK) for p in range(PAGES_PER_BLOCK)]
    q = [q_ref[0, :, h:h + 1] for h in heads]
    s_own = [jnp.sum(q[h] * kn_ref[0, :, h:h + 1], axis=0, keepdims=True) * SCALE + tab_ref[0, h] for h in heads]
    scores = []
    for h in heads:
        far = tab_ref[N_BUCKETS - 1, h]
        row = []
        for r, p in sel:
            blk = idx_ref[bi, r * N_HEADS + h]
            s = jnp.sum(kbuf[slot, tile(h, r, p)] * q[h], axis=0, keepdims=True) * SCALE
            bias = jnp.where(blk == n_blocks - 1, near_ref[h, :, p * PAGE_SIZE:(p + 1) * PAGE_SIZE], far)
            row.append(jnp.where(blk < n_blocks, s + bias, NEG))
        scores.append(row)
    m = []
    for h in heads:
        mh = scores[h][0]
        for s in scores[h][1:]:
            mh = jnp.maximum(mh, s)
        m.append(jnp.maximum(s_own[h], jnp.max(mh, axis=1, keepdims=True)))
    probs = [[jnp.exp(s - m[h]) for s in scores[h]] for h in heads]
    for h in heads:
        p_own = jnp.exp(s_own[h] - m[h])
        ps = probs[h][0]
        for p in probs[h][1:]:
            ps = ps + p
        l = p_own + jnp.sum(ps, axis=1, keepdims=True)
        acc = vbuf[slot, tile(h, 0, 0)] * probs[h][0]
        for n in range(1, len(sel)):
            acc = acc + vbuf[slot, tile(h, *sel[n])] * probs[h][n]
        o_ref[0, :, h:h + 1] = (p_own * vn_ref[0, :, h:h + 1] + jnp.sum(acc, axis=1, keepdims=True)) / l


def _decode_attn(q, k_new, v_new, cache_kt, cache_vt, page_table, idx, rel_table, near_bias):
    b, n_pages = page_table.shape
    n_blocks = n_pages // PAGES_PER_BLOCK
    n_tiles = N_HEADS * TOPK * PAGES_PER_BLOCK
    tok_spec = pl.BlockSpec((1, HEAD_DIM, N_HEADS), lambda bi, pt, ix: (bi, 0, 0))
    return pl.pallas_call(
        functools.partial(_decode_attn_body, n_blocks=n_blocks, n_pages=n_pages),
        grid_spec=pltpu.PrefetchScalarGridSpec(
            num_scalar_prefetch=2,
            grid=(b,),
            in_specs=[pl.BlockSpec(memory_space=pltpu.SMEM), tok_spec, tok_spec, tok_spec,
                      pl.BlockSpec((N_HEADS, 1, BLOCK), lambda bi, pt, ix: (0, 0, 0)),
                      pl.BlockSpec(memory_space=pl.ANY), pl.BlockSpec(memory_space=pl.ANY)],
            out_specs=tok_spec,
            scratch_shapes=[pltpu.VMEM((2, n_tiles, HEAD_DIM, PAGE_SIZE), F32),
                            pltpu.VMEM((2, n_tiles, HEAD_DIM, PAGE_SIZE), F32),
                            pltpu.SemaphoreType.DMA((2, 2))],
        ),
        out_shape=jax.ShapeDtypeStruct(q.shape, F32),
        compiler_params=_params(("arbitrary",)),
        name="decode_attn",
    )(page_table, idx, rel_table, q, k_new, v_new, near_bias, cache_kt, cache_vt)


def _pick_tile(m, target):
    return target if m % target == 0 else m


def kernel(x_prompt, x_sample, cache_k, cache_v, page_table, rel_table, g_mix, g_ffn, w_gin, g_gv, w_sp, b_sp,
           w_gout, g_kv, w_k, w_v, g_k, w_q, g_q, w_o, w_f1, w_f3, w_f2):
    bsz, seq, _ = x_prompt.shape
    dec = x_sample.shape[0]
    assert page_table.shape[1] % PAGES_PER_BLOCK == 0 and seq % BLOCK == 0
    cast = lambda w: w.astype(BF16)
    w_gin_b, w_gout_b, w_o_b = cast(w_gin[0]), cast(w_gout[0]), cast(w_o[0])
    ffn_w = (cast(w_f1), cast(w_f3), cast(w_f2))
    qkv_w = (g_mix[1], g_kv, cast(w_q[0].T), cast(w_k.T), cast(w_v.T), g_q[0], g_k)

    bias_mat, bias_near = _bias_tables(rel_table)

    xp = x_prompt.reshape(bsz * seq, D_MODEL)
    tm = _pick_tile(bsz * seq, 512)
    h = _gmlp(xp, g_mix[0], w_gin_b, g_gv[0], w_sp[0], b_sp[0], w_gout_b, single_token=False, tm=tm)
    h = _ffn(h, g_ffn[0], *ffn_w, 0, tm=tm)
    qt, vt, kb, prompt_kt, prompt_vt, means = _qkv_prompt(h.reshape(bsz, seq, D_MODEL), *qkv_w)
    attn = _moba_prompt(qt, kb, vt, means.reshape(bsz, seq // BLOCK, D_MODEL), bias_mat)
    y_prompt = _ffn(h, g_ffn[1], *ffn_w, 1, tm=tm,
                    attn=attn.reshape(bsz * seq, D_MODEL), w_o=w_o_b).reshape(bsz, seq, D_MODEL)

    cache_kt = cache_k.transpose(0, 1, 3, 2)
    cache_vt = cache_v.transpose(0, 1, 3, 2)
    xs = x_sample.reshape(dec, D_MODEL)
    hs, v_rows = _gmlp(xs, g_mix[0], w_gin_b, g_gv[0], w_sp[0], b_sp[0], w_gout_b, single_token=True, tm=dec)
    hs = _ffn(hs, g_ffn[0], *ffn_w, 0, tm=dec)
    qs, ks, vs = _qkv_decode(hs, *qkv_w)
    cmeans = _cached_block_means(cache_kt, page_table)
    idx = _decode_topk(qs.reshape(dec, 1, D_MODEL), cmeans).reshape(dec, TOPK * N_HEADS)
    as_cols = lambda a: a.reshape(dec, N_HEADS, HEAD_DIM).transpose(0, 2, 1)
    attn_s = _decode_attn(as_cols(qs), as_cols(ks), as_cols(vs), cache_kt, cache_vt, page_table, idx,
                          rel_table, bias_near)
    attn_s = attn_s.transpose(0, 2, 1).reshape(dec, D_MODEL).astype(BF16)
    y_sample = _ffn(hs, g_ffn[1], *ffn_w, 1, tm=dec,
                    attn=attn_s, w_o=w_o_b).reshape(dec, 1, D_MODEL)

    as_heads = lambda a: a.reshape(dec, N_HEADS, 1, HEAD_DIM)
    return (y_prompt, y_sample, prompt_kt.transpose(0, 1, 3, 2), prompt_vt.transpose(0, 1, 3, 2),
            as_heads(ks), as_heads(vs), v_rows.reshape(1, dec, 1, D_GATE))
```

```python
import functools
import math

import numpy as np
import jax
import jax.numpy as jnp
from jax import lax
from jax.experimental import pallas as pl
from jax.experimental.pallas import tpu as pltpu

F32 = jnp.float32
BF16 = jnp.bfloat16

D_MODEL = 1024
N_HEADS = 16
HEAD_DIM = 64
CHUNK = 128
N_GROUPS = 8
GROUP_DIM = 128
D_GATE = 1024
BLOCK = 256
TOPK = 3
N_BUCKETS = 32
MAX_DISTANCE = 128
PAGE_SIZE = 128
PAGES_PER_BLOCK = BLOCK // PAGE_SIZE
EPS = 1e-6
NEG = -1e30
FAR_BUCKET = N_BUCKETS - 1
SCALE = HEAD_DIM ** -0.5

V7X_LANES = 128
V7X_MXU_DIM = 256
V7X_VMEM_LIMIT = 56 * 1024 * 1024

LOG2E = math.log2(math.e)
FFN_CHUNK = V7X_MXU_DIM
HEADS_PER_LANE_TILE = V7X_LANES // HEAD_DIM
NORM_COLS = V7X_MXU_DIM
MEAN_BLOCKS_PER_STEP = 16
TOPK_ROWS_PER_STEP = 8
KEY_BLOCKS_PER_STEP = 2
KEY_ROWS_PER_STEP = KEY_BLOCKS_PER_STEP * BLOCK
MOBA_HEADS = 8
PV_ROWS = HEAD_DIM + 16


def _rms(x):
    return x * lax.rsqrt(jnp.mean(x * x, axis=-1, keepdims=True) + EPS)


def _gelu_tanh(x):
    c = math.sqrt(2.0 / math.pi)
    return x * (0.5 * (1.0 + jnp.tanh(c * (x + 0.044715 * (x * x * x)))))


def _dot(a, b):
    return jnp.dot(a, b, preferred_element_type=F32)


def _split_bf16(x):
    hi = x.astype(BF16)
    return hi, (x - hi.astype(F32)).astype(BF16)


def _params(sem, vmem=V7X_VMEM_LIMIT):
    return pltpu.CompilerParams(dimension_semantics=sem, vmem_limit_bytes=vmem)


def _const_spec(shape):
    zeros = (0,) * len(shape)
    return pl.BlockSpec(shape, lambda *_: zeros, pipeline_mode=pl.Buffered(1))


def _gmlp_body(x_ref, gmix_ref, win_ref, ggv_ref, wsp_ref, bsp_ref, wout_ref, *rest, single_token):
    x = x_ref[...]
    hn = (_rms(x) * gmix_ref[...]).astype(BF16)
    uv = _gelu_tanh(_dot(hn, win_ref[...]))
    u = uv[:, :D_GATE]
    v = _rms(uv[:, D_GATE:]) * ggv_ref[...]
    if single_token:
        h_ref, v_ref = rest
        v_ref[...] = v
        mixed = v * wsp_ref[...] + bsp_ref[...]
    else:
        h_ref, mixed_ref = rest
        vb = v.astype(BF16)
        row = lax.broadcasted_iota(jnp.int32, (CHUNK, CHUNK), 0)
        col = lax.broadcasted_iota(jnp.int32, (CHUNK, CHUNK), 1)
        for g in range(N_GROUPS):
            cols = slice(g * GROUP_DIM, (g + 1) * GROUP_DIM)
            w = jnp.where(row >= col, wsp_ref[g], 0.0).astype(BF16)
            for n in range(x.shape[0] // CHUNK):
                rows = slice(n * CHUNK, (n + 1) * CHUNK)
                mixed_ref[rows, cols] = _dot(w, vb[rows, cols]) + bsp_ref[:, cols]
        mixed = mixed_ref[...]
    h_ref[...] = x + _dot((u * mixed).astype(BF16), wout_ref[...])


def _gmlp(x, g_mix, w_in, g_gv, w_sp, b_sp, w_out, *, single_token, tm):
    m = x.shape[0]
    row_spec = pl.BlockSpec((tm, D_MODEL), lambda i: (i, 0))
    if single_token:
        wsp_arg = jnp.repeat(w_sp[:, 0, 0], GROUP_DIM)[None, :]
        bsp_arg = jnp.repeat(b_sp[:, 0], GROUP_DIM)[None, :]
        wsp_spec = _const_spec((1, D_GATE))
        bsp_spec = _const_spec((1, D_GATE))
        out_shape = (jax.ShapeDtypeStruct((m, D_MODEL), F32), jax.ShapeDtypeStruct((m, D_GATE), F32))
        out_specs = (row_spec, pl.BlockSpec((tm, D_GATE), lambda i: (i, 0)))
        scratch = []
    else:
        wsp_arg = w_sp
        bsp_arg = jnp.repeat(b_sp.T, GROUP_DIM, axis=1)
        wsp_spec = _const_spec((N_GROUPS, CHUNK, CHUNK))
        bsp_spec = _const_spec((CHUNK, D_GATE))
        out_shape = jax.ShapeDtypeStruct((m, D_MODEL), F32)
        out_specs = row_spec
        scratch = [pltpu.VMEM((tm, D_GATE), F32)]
    return pl.pallas_call(
        functools.partial(_gmlp_body, single_token=single_token),
        grid=(m // tm,),
        in_specs=[row_spec, _const_spec((1, D_MODEL)), _const_spec((D_MODEL, 2 * D_GATE)),
                  _const_spec((1, D_GATE)), wsp_spec, bsp_spec, _const_spec((D_GATE, D_MODEL))],
        out_specs=out_specs,
        out_shape=out_shape,
        scratch_shapes=scratch,
        compiler_params=_params(("parallel",)),
        name="gmlp_decode" if single_token else "gmlp",
    )(x, g_mix[None, :], w_in, g_gv[None, :], wsp_arg, bsp_arg, w_out)


def _ffn_body(*refs, has_proj):
    if has_proj:
        h_ref, a_ref, wo_ref, g_ref, w1_ref, w3_ref, w2_ref, o_ref = refs
        h = h_ref[...] + _dot(a_ref[...], wo_ref[...])
    else:
        h_ref, g_ref, w1_ref, w3_ref, w2_ref, o_ref = refs
        h = h_ref[...]
    n = (_rms(h) * g_ref[...]).astype(BF16)
    acc = h
    for c in range(w1_ref.shape[1] // FFN_CHUNK):
        cols = slice(c * FFN_CHUNK, (c + 1) * FFN_CHUNK)
        a = _dot(n, w1_ref[:, cols])
        b = _dot(n, w3_ref[:, cols])
        acc = acc + _dot((a * jax.nn.sigmoid(a) * b).astype(BF16), w2_ref[cols, :])
    o_ref[...] = acc


def _ffn(h, g, w1, w3, w2, layer, *, tm, attn=None, w_o=None):
    m = h.shape[0]
    d_ff = w1.shape[2]

    def layer_spec(rows, cols):
        return pl.BlockSpec((None, rows, cols), lambda *_: (layer, 0, 0), pipeline_mode=pl.Buffered(1))

    row_spec = pl.BlockSpec((tm, D_MODEL), lambda i: (i, 0))
    has_proj = attn is not None
    args = [h]
    specs = [row_spec]
    if has_proj:
        args += [attn, w_o]
        specs += [row_spec, _const_spec((D_MODEL, D_MODEL))]
    args += [g[None, :], w1, w3, w2]
    specs += [_const_spec((1, D_MODEL)), layer_spec(D_MODEL, d_ff), layer_spec(D_MODEL, d_ff),
              layer_spec(d_ff, D_MODEL)]
    return pl.pallas_call(
        functools.partial(_ffn_body, has_proj=has_proj),
        grid=(m // tm,),
        in_specs=specs,
        out_specs=row_spec,
        out_shape=jax.ShapeDtypeStruct((m, D_MODEL), F32),
        compiler_params=_params(("parallel",)),
        name="proj_ffn" if has_proj else "ffn",
    )(*args)


def _qkv_prompt_body(h_ref, gq_ref, gkv_ref, wq_ref, wk_ref, wv_ref, gqh_ref, gkh_ref,
                     qt_ref, vt_ref, kb_ref, k_ref, v_ref, mean_ref):
    n_t = _rms(h_ref[0]).T
    tok = n_t.shape[1]
    hq = (n_t * gq_ref[...]).astype(BF16)
    c = (n_t * gkv_ref[...]).astype(BF16)

    def head_norm(y, g_ref):
        y3 = y.reshape(N_HEADS, HEAD_DIM, tok)
        ms = jnp.mean(y3 * y3, axis=1, keepdims=True)
        return y3 * lax.rsqrt(ms + EPS) * g_ref[...]

    q3 = head_norm(_dot(wq_ref[...], hq), gqh_ref)
    k3 = head_norm(_dot(wk_ref[...], c), gkh_ref)
    v = _dot(wv_ref[...], c)
    qt_ref[0] = (q3 * (SCALE * LOG2E)).reshape(D_MODEL, tok).astype(BF16)
    v3 = v.reshape(N_HEADS, HEAD_DIM, tok)
    ones = jnp.ones((N_HEADS, PV_ROWS - HEAD_DIM, tok), F32)
    vt_ref[0, 0] = jnp.concatenate([v3, ones], axis=1).reshape(N_HEADS * PV_ROWS, tok).astype(BF16)
    k_ref[0] = k3
    v_ref[0] = v3
    k_tok = k3.reshape(D_MODEL, tok).T
    kb_ref[0] = k_tok.astype(BF16)
    mean_ref[0, 0] = jnp.mean(k_tok, axis=0, keepdims=True)


def _qkv_prompt(h, g_q_in, g_kv, w_q, w_k, w_v, g_q, g_k):
    b, s, _ = h.shape
    nb = s // BLOCK
    col = lambda g: g[:, None]
    args = [col(g_q_in), col(g_kv), w_q, w_k, w_v, col(g_q), col(g_k)]
    specs = ([_const_spec((D_MODEL, 1))] * 2 + [_const_spec((D_MODEL, D_MODEL))] * 3
             + [_const_spec((HEAD_DIM, 1))] * 2)
    head_spec = pl.BlockSpec((1, N_HEADS, HEAD_DIM, BLOCK), lambda bi, i: (bi, 0, 0, i))
    head_shape = jax.ShapeDtypeStruct((b, N_HEADS, HEAD_DIM, s), F32)
    return pl.pallas_call(
        _qkv_prompt_body,
        grid=(b, nb),
        in_specs=[pl.BlockSpec((1, BLOCK, D_MODEL), lambda bi, i: (bi, i, 0))] + specs,
        out_specs=(pl.BlockSpec((1, D_MODEL, BLOCK), lambda bi, i: (bi, 0, i)),
                   pl.BlockSpec((1, 1, N_HEADS * PV_ROWS, BLOCK), lambda bi, i: (bi, i, 0, 0)),
                   pl.BlockSpec((1, BLOCK, D_MODEL), lambda bi, i: (bi, i, 0)),
                   head_spec, head_spec,
                   pl.BlockSpec((1, 1, 1, D_MODEL), lambda bi, i: (bi, i, 0, 0))),
        out_shape=(jax.ShapeDtypeStruct((b, D_MODEL, s), BF16),
                   jax.ShapeDtypeStruct((b, nb, N_HEADS * PV_ROWS, BLOCK), BF16),
                   jax.ShapeDtypeStruct((b, s, D_MODEL), BF16),
                   head_shape, head_shape,
                   jax.ShapeDtypeStruct((b, nb, 1, D_MODEL), F32)),
        compiler_params=_params(("parallel", "parallel")),
        name="qkv",
    )(h, *args)


def _qkv_decode_body(h_ref, gq_ref, gkv_ref, wq_ref, wk_ref, wv_ref, gqh_ref, gkh_ref, grp_ref,
                     q_ref, k_ref, v_ref):
    n = _rms(h_ref[...])
    hq = (n * gq_ref[...]).astype(BF16)
    c = (n * gkv_ref[...]).astype(BF16)

    def head_norm(y, g_ref):
        pieces = []
        for cb in range(D_MODEL // NORM_COLS):
            ys = y[:, cb * NORM_COLS:(cb + 1) * NORM_COLS]
            ms = _dot((ys * ys).astype(BF16), grp_ref[...])
            pieces.append(ys * lax.rsqrt(ms + EPS))
        return jnp.concatenate(pieces, axis=1) * g_ref[...]

    def project(x, w_ref):
        return lax.dot_general(x, w_ref[...], (((1,), (1,)), ((), ())), preferred_element_type=F32)

    q_ref[...] = head_norm(project(hq, wq_ref), gqh_ref)
    k_ref[...] = head_norm(project(c, wk_ref), gkh_ref)
    v_ref[...] = project(c, wv_ref)


def _qkv_decode(h, g_q_in, g_kv, w_q, w_k, w_v, g_q, g_k):
    m = h.shape[0]
    head = np.arange(NORM_COLS) // HEAD_DIM
    grp = jnp.asarray((head[:, None] == head[None, :]).astype(np.float32) / HEAD_DIM, dtype=BF16)
    args = [g_q_in[None, :], g_kv[None, :], w_q, w_k, w_v,
            jnp.tile(g_q, N_HEADS)[None, :], jnp.tile(g_k, N_HEADS)[None, :], grp]
    specs = ([_const_spec((1, D_MODEL))] * 2 + [_const_spec((D_MODEL, D_MODEL))] * 3
             + [_const_spec((1, D_MODEL))] * 2 + [_const_spec((NORM_COLS, NORM_COLS))])
    row_spec = pl.BlockSpec((m, D_MODEL), lambda i: (0, 0))
    shape = jax.ShapeDtypeStruct((m, D_MODEL), F32)
    return pl.pallas_call(
        _qkv_decode_body,
        grid=(1,),
        in_specs=[row_spec] + specs,
        out_specs=(row_spec, row_spec, row_spec),
        out_shape=(shape, shape, shape),
        compiler_params=_params(("arbitrary",)),
        name="qkv_decode",
    )(h, *args)


def _bucket_of_distance(n):
    n = np.asarray(n, dtype=np.int64)
    max_exact = N_BUCKETS // 2
    nf = np.maximum(n, 1).astype(np.float32)
    large = max_exact + (np.log(nf / np.float32(max_exact)) / np.float32(math.log(MAX_DISTANCE / max_exact))
                         * np.float32(N_BUCKETS - max_exact)).astype(np.int32)
    large = np.minimum(large, N_BUCKETS - 1)
    return np.where(n < max_exact, n, large).astype(np.int32)


def _bias_body(tab_ref, bm_ref, bv_ref, mat_ref, vec_ref):
    h = pl.program_id(0)
    mat = jnp.full(bm_ref.shape, NEG, F32)
    vec = jnp.full(bv_ref.shape, NEG, F32)
    for bkt in range(N_BUCKETS):
        val = tab_ref[bkt, h]
        mat = jnp.where(bm_ref[...] == bkt, val * LOG2E, mat)
        vec = jnp.where(bv_ref[...] == bkt, val, vec)
    mat_ref[0, 0] = jnp.full(bm_ref.shape[1:], tab_ref[FAR_BUCKET, h] * LOG2E, F32)
    mat_ref[0, 1:] = mat
    vec_ref[0] = vec


def _bias_tables(rel_table):
    key = np.arange(BLOCK)[:, None]
    qry = np.arange(BLOCK)[None, :]
    own = np.where(qry >= key, _bucket_of_distance(qry - key), -1)
    adj = _bucket_of_distance(BLOCK + qry - key)
    assert _bucket_of_distance(BLOCK + 1) == FAR_BUCKET
    bm = jnp.asarray(np.stack([adj, own]).astype(np.int32))
    bv = jnp.asarray(_bucket_of_distance(BLOCK - np.arange(BLOCK))[None, :].astype(np.int32))
    return pl.pallas_call(
        _bias_body,
        grid=(N_HEADS,),
        in_specs=[pl.BlockSpec(memory_space=pltpu.SMEM), _const_spec((2, BLOCK, BLOCK)), _const_spec((1, BLOCK))],
        out_specs=(pl.BlockSpec((1, 3, BLOCK, BLOCK), lambda h: (h, 0, 0, 0)),
                   pl.BlockSpec((1, 1, BLOCK), lambda h: (h, 0, 0))),
        out_shape=(jax.ShapeDtypeStruct((N_HEADS, 3, BLOCK, BLOCK), F32),
                   jax.ShapeDtypeStruct((N_HEADS, 1, BLOCK), F32)),
        compiler_params=_params(("arbitrary",)),
        name="rel_bias",
    )(rel_table, bm, bv)


def _top_blocks_penalty(gate, n_valid, own):
    nb = gate.shape[0]
    blk = lax.broadcasted_iota(jnp.int32, gate.shape, 0).astype(F32)
    valid = blk < n_valid
    g = jnp.where(valid, gate, NEG)
    sel = jnp.zeros(gate.shape, jnp.bool_)
    for _ in range(TOPK):
        mx = jnp.max(g, axis=0, keepdims=True)
        first = jnp.min(jnp.where(g == mx, blk, float(nb)), axis=0, keepdims=True)
        pick = blk == first
        sel = jnp.logical_or(sel, pick)
        g = jnp.where(pick, -jnp.inf, g)
    sel = jnp.logical_or(jnp.logical_and(sel, valid), blk == own)
    return jnp.where(sel, 0.0, NEG)


def _moba_body(qt_ref, k_ref, vt_ref, mean_ref, bias_ref, o_ref, qm_ref, pen_ref, sa_ref, sb_ref):
    i = pl.program_id(2)
    heads = range(MOBA_HEADS)
    subs = range(KEY_BLOCKS_PER_STEP)

    def lanes(t):
        lt = t // HEADS_PER_LANE_TILE
        return slice(lt * V7X_LANES, (lt + 1) * V7X_LANES)

    feat = lax.broadcasted_iota(jnp.int32, (V7X_LANES, BLOCK), 0)
    for t in heads:
        half = t % HEADS_PER_LANE_TILE
        in_head = jnp.logical_and(feat >= half * HEAD_DIM, feat < (half + 1) * HEAD_DIM)
        qt = qt_ref[0, lanes(t), :]
        qm_ref[t] = jnp.where(in_head, qt, jnp.zeros_like(qt))

    def rows(j):
        return pl.ds(pl.multiple_of(j * BLOCK, BLOCK), BLOCK)

    def head_rows(t):
        return slice(t * PV_ROWS, (t + 1) * PV_ROWS)

    def col_max(x):
        return jnp.max(x, axis=0, keepdims=True)


    def score_near(buf):
        j_adj = jnp.maximum(i - 1, 0)
        no_adj = jnp.where(i >= 1, 0.0, NEG)
        s_adj = [_dot(k_ref[0, rows(j_adj), lanes(t)], qm_ref[t]) for t in heads]
        s_own = [_dot(k_ref[0, rows(i), lanes(t)], qm_ref[t]) for t in heads]
        i_f = i.astype(F32)
        for t in heads:
            mean_hi, mean_lo = _split_bf16(mean_ref[0, :, lanes(t)])
            gate = _dot(mean_hi, qm_ref[t]) + _dot(mean_lo, qm_ref[t])
            pen_ref[t] = _top_blocks_penalty(gate, i_f, i_f)
        meta = []
        for t in heads:
            tiles = [s_adj[t] + bias_ref[t, 1], s_own[t] + bias_ref[t, 2]]
            offs = [pen_ref[t, pl.ds(j_adj, 1), :] + no_adj, jnp.zeros((1, BLOCK), F32)]
            for sub in subs:
                buf[t, sub] = tiles[sub]
            meta.append((offs, jnp.maximum(col_max(tiles[0]) + offs[0], col_max(tiles[1]))))
        return tuple(meta), (j_adj, i)

    n_far = jnp.maximum(i - 1, 0)

    def score_far(c, buf):
        key_rows = pl.ds(pl.multiple_of(c * KEY_ROWS_PER_STEP, KEY_ROWS_PER_STEP), KEY_ROWS_PER_STEP)
        blocks = tuple(c * KEY_BLOCKS_PER_STEP + sub for sub in subs)
        scores = [_dot(k_ref[0, key_rows, lanes(t)], qm_ref[t]) for t in heads]
        meta = []
        for t in heads:
            far_bias = bias_ref[t, 0, 0:1, 0:1]
            tiles = [scores[t][sub * BLOCK:(sub + 1) * BLOCK] for sub in subs]
            offs = [pen_ref[t, pl.ds(blocks[sub], 1), :] + far_bias + jnp.where(blocks[sub] < n_far, 0.0, NEG)
                    for sub in subs]
            cm = col_max(tiles[0]) + offs[0]
            for sub in subs[1:]:
                cm = jnp.maximum(cm, col_max(tiles[sub]) + offs[sub])
            for sub in subs:
                buf[t, sub] = tiles[sub]
            meta.append((offs, cm))
        return tuple(meta), blocks

    def fold(state, buf, meta, blocks):
        new = []
        for t in heads:
            m, acc = state[t]
            offs, cm = meta[t]
            m_new = jnp.maximum(m, cm)
            acc = jnp.exp2(m - m_new) * acc
            for sub in subs:
                p = jnp.exp2(buf[t, sub] - (m_new - offs[sub]))
                acc = acc + _dot(vt_ref[0, blocks[sub], head_rows(t), :], p.astype(BF16))
            new.append((m_new, acc))
        return tuple(new)

    def step(c2, carry):
        state, meta, blocks = carry
        meta_b, blocks_b = score_far(2 * c2, sb_ref)
        state = fold(state, sa_ref, meta, blocks)
        meta_a, blocks_a = score_far(2 * c2 + 1, sa_ref)
        state = fold(state, sb_ref, meta_b, blocks_b)
        return state, meta_a, blocks_a

    state = tuple((jnp.full((1, BLOCK), -jnp.inf, F32), jnp.zeros((PV_ROWS, BLOCK), F32)) for _ in heads)
    n_units = (n_far + KEY_BLOCKS_PER_STEP - 1) // KEY_BLOCKS_PER_STEP
    meta, blocks = score_near(sa_ref)
    state, meta, blocks = lax.fori_loop(0, n_units // 2, step, (state, meta, blocks))

    def last_with_odd_unit(args):
        state, meta, blocks = args
        meta_b, blocks_b = score_far(n_units - 1, sb_ref)
        return fold(fold(state, sa_ref, meta, blocks), sb_ref, meta_b, blocks_b)

    def last(args):
        state, meta, blocks = args
        return fold(state, sa_ref, meta, blocks)

    state = lax.cond(n_units % 2 == 1, last_with_odd_unit, last, (state, meta, blocks))
    ot = jnp.concatenate([acc[:HEAD_DIM] / acc[HEAD_DIM:HEAD_DIM + 1] for (_, acc) in state], axis=0)
    o_ref[0] = ot.T.astype(o_ref.dtype)


def _moba_prompt(qt, kb, vt, means, bias_mat):
    b, s, _ = kb.shape
    nb = s // BLOCK
    assert pl.cdiv(max(nb - 2, 0), KEY_BLOCKS_PER_STEP) * KEY_BLOCKS_PER_STEP <= nb
    width = MOBA_HEADS * HEAD_DIM
    return pl.pallas_call(
        _moba_body,
        grid=(b, D_MODEL // width, nb),
        in_specs=[pl.BlockSpec((1, width, BLOCK), lambda bi, p, i: (bi, p, i)),
                  pl.BlockSpec((1, s, width), lambda bi, p, i: (bi, 0, p)),
                  pl.BlockSpec((1, nb, MOBA_HEADS * PV_ROWS, BLOCK), lambda bi, p, i: (bi, 0, p, 0)),
                  pl.BlockSpec((1, nb, width), lambda bi, p, i: (bi, 0, p)),
                  pl.BlockSpec((MOBA_HEADS, 3, BLOCK, BLOCK), lambda bi, p, i: (p, 0, 0, 0))],
        out_specs=pl.BlockSpec((1, BLOCK, width), lambda bi, p, i: (bi, i, p)),
        out_shape=jax.ShapeDtypeStruct((b, s, D_MODEL), BF16),
        scratch_shapes=[pltpu.VMEM((MOBA_HEADS, V7X_LANES, BLOCK), BF16),
                        pltpu.VMEM((MOBA_HEADS, nb, BLOCK), F32)]
        + [pltpu.VMEM((MOBA_HEADS, KEY_BLOCKS_PER_STEP, BLOCK, BLOCK), F32)] * 2,
        compiler_params=_params(("parallel", "parallel", "arbitrary")),
        name="moba_prompt",
    )(qt, kb, vt, means, bias_mat)


def _page_mean_body(pt_ref, *refs):
    del pt_ref
    pages, o_ref = refs[:-1], refs[-1]
    ones = jnp.ones((8, PAGE_SIZE), BF16)
    contract_minor = (((1,), (1,)), ((), ()))
    rows = []
    for t in range(len(pages) // PAGES_PER_BLOCK):
        tot = pages[PAGES_PER_BLOCK * t][0]
        for p in range(1, PAGES_PER_BLOCK):
            tot = tot + pages[PAGES_PER_BLOCK * t + p][0]
        hi, lo = _split_bf16(tot.reshape(D_MODEL, PAGE_SIZE))
        sums = (lax.dot_general(ones, hi, contract_minor, preferred_element_type=F32)
                + lax.dot_general(ones, lo, contract_minor, preferred_element_type=F32))
        rows.append(sums[0:1])
    o_ref[0] = jnp.concatenate(rows, axis=0) * (1.0 / BLOCK)


def _cached_block_means(cache_kt, page_table):
    b, n_pages = page_table.shape
    n_blocks = n_pages // PAGES_PER_BLOCK
    assert n_blocks % MEAN_BLOCKS_PER_STEP == 0
    steps = n_blocks // MEAN_BLOCKS_PER_STEP
    pages_per_step = MEAN_BLOCKS_PER_STEP * PAGES_PER_BLOCK

    def page_spec(t):
        return pl.BlockSpec((1, N_HEADS, HEAD_DIM, PAGE_SIZE),
                            lambda bi, g, pt: (pt[bi, g * pages_per_step + t], 0, 0, 0))

    return pl.pallas_call(
        _page_mean_body,
        grid_spec=pltpu.PrefetchScalarGridSpec(
            num_scalar_prefetch=1,
            grid=(b, steps),
            in_specs=[page_spec(t) for t in range(pages_per_step)],
            out_specs=pl.BlockSpec((1, MEAN_BLOCKS_PER_STEP, D_MODEL), lambda bi, g, pt: (bi, g, 0)),
        ),
        out_shape=jax.ShapeDtypeStruct((b, n_blocks, D_MODEL), F32),
        compiler_params=_params(("parallel", "parallel")),
        name="page_means",
    )(page_table, *([cache_kt] * pages_per_step))


def _decode_topk_body(q_ref, cm_ref, seg_ref, idx_ref):
    n_blocks = cm_ref.shape[1]
    blk = lax.broadcasted_iota(jnp.int32, (n_blocks, N_HEADS), 0).astype(F32)
    for r in range(cm_ref.shape[0]):
        hi, lo = _split_bf16(cm_ref[r] * q_ref[r])
        gate = _dot(hi, seg_ref[...]) + _dot(lo, seg_ref[...])
        rows = []
        for _ in range(TOPK):
            mx = jnp.max(gate, axis=0, keepdims=True)
            first = jnp.min(jnp.where(gate == mx, blk, float(n_blocks)), axis=0, keepdims=True)
            rows.append(first)
            gate = jnp.where(blk == first, -jnp.inf, gate)
        idx_ref[r] = jnp.concatenate(rows, axis=0).astype(jnp.int32)


def _decode_topk(q, cmeans):
    b, n_blocks, _ = cmeans.shape
    rows = _pick_tile(b, TOPK_ROWS_PER_STEP)
    seg = jnp.asarray((np.arange(D_MODEL)[:, None] // HEAD_DIM == np.arange(N_HEADS)[None, :]).astype(np.float32),
                      dtype=BF16)
    return pl.pallas_call(
        _decode_topk_body,
        grid=(b // rows,),
        in_specs=[pl.BlockSpec((rows, 1, D_MODEL), lambda bi: (bi, 0, 0)),
                  pl.BlockSpec((rows, n_blocks, D_MODEL), lambda bi: (bi, 0, 0)),
                  _const_spec((D_MODEL, N_HEADS))],
        out_specs=pl.BlockSpec((rows, TOPK, N_HEADS), lambda bi: (bi, 0, 0)),
        out_shape=jax.ShapeDtypeStruct((b, TOPK, N_HEADS), jnp.int32),
        compiler_params=_params(("parallel",)),
        name="decode_topk",
    )(q, cmeans, seg)


def _decode_attn_body(pt_ref, idx_ref, tab_ref, q_ref, kn_ref, vn_ref, near_ref, ck_ref, cv_ref, o_ref,
                      kbuf, vbuf, sem, *, n_blocks, n_pages):
    bi = pl.program_id(0)

    def tile(h, r, p):
        return (h * TOPK + r) * PAGES_PER_BLOCK + p

    def gather(row, slot, action):
        def per_head(h, carry):
            for r in range(TOPK):
                blk = idx_ref[row, r * N_HEADS + h]
                for p in range(PAGES_PER_BLOCK):
                    logical = jnp.minimum(blk * PAGES_PER_BLOCK + p, n_pages - 1)
                    phys = pt_ref[row, logical]
                    n = tile(h, r, p)
                    action(pltpu.make_async_copy(ck_ref.at[phys, h], kbuf.at[slot, n], sem.at[0, slot]))
                    action(pltpu.make_async_copy(cv_ref.at[phys, h], vbuf.at[slot, n], sem.at[1, slot]))
            return carry
        lax.fori_loop(0, N_HEADS, per_head, 0)

    slot = bi % 2

    @pl.when(bi == 0)
    def _():
        gather(bi, slot, lambda c: c.start())

    @pl.when(bi + 1 < pl.num_programs(0))
    def _():
        gather(bi + 1, 1 - slot, lambda c: c.start())

    gather(bi, slot, lambda c: c.wait())

    heads = range(N_HEADS)
    sel = [(r, p) for r in range(TOPK) for p in range(PAGES_PER_BLOCK)]
    q = [q_ref[0, :, h:h + 1] for h in heads]
    s_own = [jnp.sum(q[h] * kn_ref[0, :, h:h + 1], axis=0, keepdims=True) * SCALE + tab_ref[0, h] for h in heads]
    scores = []
    for h in heads:
        far = tab_ref[N_BUCKETS - 1, h]
        row = []
        for r, p in sel:
            blk = idx_ref[bi, r * N_HEADS + h]
            s = jnp.sum(kbuf[slot, tile(h, r, p)] * q[h], axis=0, keepdims=True) * SCALE
            bias = jnp.where(blk == n_blocks - 1, near_ref[h, :, p * PAGE_SIZE:(p + 1) * PAGE_SIZE], far)
            row.append(jnp.where(blk < n_blocks, s + bias, NEG))
        scores.append(row)
    m = []
    for h in heads:
        mh = scores[h][0]
        for s in scores[h][1:]:
            mh = jnp.maximum(mh, s)
        m.append(jnp.maximum(s_own[h], jnp.max(mh, axis=1, keepdims=True)))
    probs = [[jnp.exp(s - m[h]) for s in scores[h]] for h in heads]
    for h in heads:
        p_own = jnp.exp(s_own[h] - m[h])
        ps = probs[h][0]
        for p in probs[h][1:]:
            ps = ps + p
        l = p_own + jnp.sum(ps, axis=1, keepdims=True)
        acc = vbuf[slot, tile(h, 0, 0)] * probs[h][0]
        for n in range(1, len(sel)):
            acc = acc + vbuf[slot, tile(h, *sel[n])] * probs[h][n]
        o_ref[0, :, h:h + 1] = (p_own * vn_ref[0, :, h:h + 1] + jnp.sum(acc, axis=1, keepdims=True)) / l


def _decode_attn(q, k_new, v_new, cache_kt, cache_vt, page_table, idx, rel_table, near_bias):
    b, n_pages = page_table.shape
    n_blocks = n_pages // PAGES_PER_BLOCK
    n_tiles = N_HEADS * TOPK * PAGES_PER_BLOCK
    tok_spec = pl.BlockSpec((1, HEAD_DIM, N_HEADS), lambda bi, pt, ix: (bi, 0, 0))
    return pl.pallas_call(
        functools.partial(_decode_attn_body, n_blocks=n_blocks, n_pages=n_pages),
        grid_spec=pltpu.PrefetchScalarGridSpec(
            num_scalar_prefetch=2,
            grid=(b,),
            in_specs=[pl.BlockSpec(memory_space=pltpu.SMEM), tok_spec, tok_spec, tok_spec,
                      pl.BlockSpec((N_HEADS, 1, BLOCK), lambda bi, pt, ix: (0, 0, 0)),
                      pl.BlockSpec(memory_space=pl.ANY), pl.BlockSpec(memory_space=pl.ANY)],
            out_specs=tok_spec,
            scratch_shapes=[pltpu.VMEM((2, n_tiles, HEAD_DIM, PAGE_SIZE), F32),
                            pltpu.VMEM((2, n_tiles, HEAD_DIM, PAGE_SIZE), F32),
                            pltpu.SemaphoreType.DMA((2, 2))],
        ),
        out_shape=jax.ShapeDtypeStruct(q.shape, F32),
        compiler_params=_params(("arbitrary",)),
        name="decode_attn",
    )(page_table, idx, rel_table, q, k_new, v_new, near_bias, cache_kt, cache_vt)


def _pick_tile(m, target):
    return target if m % target == 0 else m


def kernel(x_prompt, x_sample, cache_k, cache_v, page_table, rel_table, g_mix, g_ffn, w_gin, g_gv, w_sp, b_sp,
           w_gout, g_kv, w_k, w_v, g_k, w_q, g_q, w_o, w_f1, w_f3, w_f2):
    bsz, seq, _ = x_prompt.shape
    dec = x_sample.shape[0]
    assert page_table.shape[1] % PAGES_PER_BLOCK == 0 and seq % BLOCK == 0
    cast = lambda w: w.astype(BF16)
    w_gin_b, w_gout_b, w_o_b = cast(w_gin[0]), cast(w_gout[0]), cast(w_o[0])
    ffn_w = (cast(w_f1), cast(w_f3), cast(w_f2))
    qkv_w = (g_mix[1], g_kv, cast(w_q[0].T), cast(w_k.T), cast(w_v.T), g_q[0], g_k)

    bias_mat, bias_near = _bias_tables(rel_table)

    xp = x_prompt.reshape(bsz * seq, D_MODEL)
    tm = _pick_tile(bsz * seq, 512)
    h = _gmlp(xp, g_mix[0], w_gin_b, g_gv[0], w_sp[0], b_sp[0], w_gout_b, single_token=False, tm=tm)
    h = _ffn(h, g_ffn[0], *ffn_w, 0, tm=tm)
    qt, vt, kb, prompt_kt, prompt_vt, means = _qkv_prompt(h.reshape(bsz, seq, D_MODEL), *qkv_w)
    attn = _moba_prompt(qt, kb, vt, means.reshape(bsz, seq // BLOCK, D_MODEL), bias_mat)
    y_prompt = _ffn(h, g_ffn[1], *ffn_w, 1, tm=tm,
                    attn=attn.reshape(bsz * seq, D_MODEL), w_o=w_o_b).reshape(bsz, seq, D_MODEL)

    cache_kt = cache_k.transpose(0, 1, 3, 2)
    cache_vt = cache_v.transpose(0, 1, 3, 2)
    xs = x_sample.reshape(dec, D_MODEL)
    hs, v_rows = _gmlp(xs, g_mix[0], w_gin_b, g_gv[0], w_sp[0], b_sp[0], w_gout_b, single_token=True, tm=dec)
    hs = _ffn(hs, g_ffn[0], *ffn_w, 0, tm=dec)
    qs, ks, vs = _qkv_decode(hs, *qkv_w)
    cmeans = _cached_block_means(cache_kt, page_table)
    idx = _decode_topk(qs.reshape(dec, 1, D_MODEL), cmeans).reshape(dec, TOPK * N_HEADS)
    as_cols = lambda a: a.reshape(dec, N_HEADS, HEAD_DIM).transpose(0, 2, 1)
    attn_s = _decode_attn(as_cols(qs), as_cols(ks), as_cols(vs), cache_kt, cache_vt, page_table, idx,
                          rel_table, bias_near)
    attn_s = attn_s.transpose(0, 2, 1).reshape(dec, D_MODEL).astype(BF16)
    y_sample = _ffn(hs, g_ffn[1], *ffn_w, 1, tm=dec,
                    attn=attn_s, w_o=w_o_b).reshape(dec, 1, D_MODEL)

    as_heads = lambda a: a.reshape(dec, N_HEADS, 1, HEAD_DIM)
    return (y_prompt, y_sample, prompt_kt.transpose(0, 1, 3, 2), prompt_vt.transpose(0, 1, 3, 2),
            as_heads(ks), as_heads(vs), v_rows.reshape(1, dec, 1, D_GATE))
```

```python
import functools
import math

import numpy as np
import jax
import jax.numpy as jnp
from jax import lax
from jax.experimental import pallas as pl
from jax.experimental.pallas import tpu as pltpu

F32 = jnp.float32
BF16 = jnp.bfloat16

D_MODEL = 1024
N_HEADS = 16
HEAD_DIM = 64
CHUNK = 128
N_GROUPS = 8
GROUP_DIM = 128
D_GATE = 1024
BLOCK = 256
TOPK = 3
N_BUCKETS = 32
MAX_DISTANCE = 128
PAGE_SIZE = 128
PAGES_PER_BLOCK = BLOCK // PAGE_SIZE
EPS = 1e-6
NEG = -1e30
FAR_BUCKET = N_BUCKETS - 1
SCALE = HEAD_DIM ** -0.5

V7X_LANES = 128
V7X_MXU_DIM = 256
V7X_VMEM_LIMIT = 56 * 1024 * 1024

LOG2E = math.log2(math.e)
FFN_CHUNK = V7X_MXU_DIM
HEADS_PER_LANE_TILE = V7X_LANES // HEAD_DIM
NORM_COLS = V7X_MXU_DIM
MEAN_BLOCKS_PER_STEP = 16
TOPK_ROWS_PER_STEP = 8
KEY_BLOCKS_PER_STEP = 2
KEY_ROWS_PER_STEP = KEY_BLOCKS_PER_STEP * BLOCK
MOBA_HEADS = 8
PV_ROWS = HEAD_DIM + 16
QKV_BLOCKS_PER_STEP = 2


def _rms(x):
    return x * lax.rsqrt(jnp.mean(x * x, axis=-1, keepdims=True) + EPS)


def _gelu_tanh(x):
    c = math.sqrt(2.0 / math.pi)
    return x * (0.5 * (1.0 + jnp.tanh(c * (x + 0.044715 * (x * x * x)))))


def _dot(a, b):
    return jnp.dot(a, b, preferred_element_type=F32)


def _split_bf16(x):
    hi = x.astype(BF16)
    return hi, (x - hi.astype(F32)).astype(BF16)


def _params(sem, vmem=V7X_VMEM_LIMIT):
    return pltpu.CompilerParams(dimension_semantics=sem, vmem_limit_bytes=vmem)


def _const_spec(shape):
    zeros = (0,) * len(shape)
    return pl.BlockSpec(shape, lambda *_: zeros, pipeline_mode=pl.Buffered(1))


def _gmlp_body(x_ref, gmix_ref, win_ref, ggv_ref, wsp_ref, bsp_ref, wout_ref, *rest, single_token):
    x = x_ref[...]
    hn = (_rms(x) * gmix_ref[...]).astype(BF16)
    uv = _gelu_tanh(_dot(hn, win_ref[...]))
    u = uv[:, :D_GATE]
    v = _rms(uv[:, D_GATE:]) * ggv_ref[...]
    if single_token:
        h_ref, v_ref = rest
        v_ref[...] = v
        mixed = v * wsp_ref[...] + bsp_ref[...]
    else:
        h_ref, mixed_ref = rest
        vb = v.astype(BF16)
        row = lax.broadcasted_iota(jnp.int32, (CHUNK, CHUNK), 0)
        col = lax.broadcasted_iota(jnp.int32, (CHUNK, CHUNK), 1)
        for g in range(N_GROUPS):
            cols = slice(g * GROUP_DIM, (g + 1) * GROUP_DIM)
            w = jnp.where(row >= col, wsp_ref[g], 0.0).astype(BF16)
            for n in range(x.shape[0] // CHUNK):
                rows = slice(n * CHUNK, (n + 1) * CHUNK)
                mixed_ref[rows, cols] = _dot(w, vb[rows, cols]) + bsp_ref[:, cols]
        mixed = mixed_ref[...]
    h_ref[...] = x + _dot((u * mixed).astype(BF16), wout_ref[...])


def _gmlp(x, g_mix, w_in, g_gv, w_sp, b_sp, w_out, *, single_token, tm):
    m = x.shape[0]
    row_spec = pl.BlockSpec((tm, D_MODEL), lambda i: (i, 0))
    if single_token:
        wsp_arg = jnp.repeat(w_sp[:, 0, 0], GROUP_DIM)[None, :]
        bsp_arg = jnp.repeat(b_sp[:, 0], GROUP_DIM)[None, :]
        wsp_spec = _const_spec((1, D_GATE))
        bsp_spec = _const_spec((1, D_GATE))
        out_shape = (jax.ShapeDtypeStruct((m, D_MODEL), F32), jax.ShapeDtypeStruct((m, D_GATE), F32))
        out_specs = (row_spec, pl.BlockSpec((tm, D_GATE), lambda i: (i, 0)))
        scratch = []
    else:
        wsp_arg = w_sp
        bsp_arg = jnp.repeat(b_sp.T, GROUP_DIM, axis=1)
        wsp_spec = _const_spec((N_GROUPS, CHUNK, CHUNK))
        bsp_spec = _const_spec((CHUNK, D_GATE))
        out_shape = jax.ShapeDtypeStruct((m, D_MODEL), F32)
        out_specs = row_spec
        scratch = [pltpu.VMEM((tm, D_GATE), F32)]
    return pl.pallas_call(
        functools.partial(_gmlp_body, single_token=single_token),
        grid=(m // tm,),
        in_specs=[row_spec, _const_spec((1, D_MODEL)), _const_spec((D_MODEL, 2 * D_GATE)),
                  _const_spec((1, D_GATE)), wsp_spec, bsp_spec, _const_spec((D_GATE, D_MODEL))],
        out_specs=out_specs,
        out_shape=out_shape,
        scratch_shapes=scratch,
        compiler_params=_params(("parallel",)),
        name="gmlp_decode" if single_token else "gmlp",
    )(x, g_mix[None, :], w_in, g_gv[None, :], wsp_arg, bsp_arg, w_out)


def _ffn_body(*refs, has_proj):
    if has_proj:
        h_ref, a_ref, wo_ref, g_ref, w1_ref, w3_ref, w2_ref, o_ref = refs
        h = h_ref[...] + _dot(a_ref[...], wo_ref[...])
    else:
        h_ref, g_ref, w1_ref, w3_ref, w2_ref, o_ref = refs
        h = h_ref[...]
    n = (_rms(h) * g_ref[...]).astype(BF16)
    acc = h
    for c in range(w1_ref.shape[1] // FFN_CHUNK):
        cols = slice(c * FFN_CHUNK, (c + 1) * FFN_CHUNK)
        a = _dot(n, w1_ref[:, cols])
        b = _dot(n, w3_ref[:, cols])
        acc = acc + _dot((a * jax.nn.sigmoid(a) * b).astype(BF16), w2_ref[cols, :])
    o_ref[...] = acc


def _ffn(h, g, w1, w3, w2, layer, *, tm, attn=None, w_o=None):
    m = h.shape[0]
    d_ff = w1.shape[2]

    def layer_spec(rows, cols):
        return pl.BlockSpec((None, rows, cols), lambda *_: (layer, 0, 0), pipeline_mode=pl.Buffered(1))

    row_spec = pl.BlockSpec((tm, D_MODEL), lambda i: (i, 0))
    has_proj = attn is not None
    args = [h]
    specs = [row_spec]
    if has_proj:
        args += [attn, w_o]
        specs += [row_spec, _const_spec((D_MODEL, D_MODEL))]
    args += [g[None, :], w1, w3, w2]
    specs += [_const_spec((1, D_MODEL)), layer_spec(D_MODEL, d_ff), layer_spec(D_MODEL, d_ff),
              layer_spec(d_ff, D_MODEL)]
    return pl.pallas_call(
        functools.partial(_ffn_body, has_proj=has_proj),
        grid=(m // tm,),
        in_specs=specs,
        out_specs=row_spec,
        out_shape=jax.ShapeDtypeStruct((m, D_MODEL), F32),
        compiler_params=_params(("parallel",)),
        name="proj_ffn" if has_proj else "ffn",
    )(*args)


def _qkv_prompt_body(h_ref, gq_ref, gkv_ref, wq_ref, wk_ref, wv_ref, gqh_ref, gkh_ref,
                     qt_ref, vt_ref, kb_ref, k_ref, v_ref, mean_ref):
    n_t = _rms(h_ref[0]).T
    tok = n_t.shape[1]
    hq = (n_t * gq_ref[...]).astype(BF16)
    c = (n_t * gkv_ref[...]).astype(BF16)

    def head_norm(y, g_ref):
        y3 = y.reshape(N_HEADS, HEAD_DIM, tok)
        ms = jnp.mean(y3 * y3, axis=1, keepdims=True)
        return y3 * lax.rsqrt(ms + EPS) * g_ref[...]

    q3 = head_norm(_dot(wq_ref[...], hq), gqh_ref)
    k3 = head_norm(_dot(wk_ref[...], c), gkh_ref)
    v = _dot(wv_ref[...], c)
    qt_ref[0] = (q3 * (SCALE * LOG2E)).reshape(D_MODEL, tok).astype(BF16)
    v3 = v.reshape(N_HEADS, HEAD_DIM, tok)
    ones = jnp.ones((N_HEADS, PV_ROWS - HEAD_DIM, tok), F32)
    vt = jnp.concatenate([v3, ones], axis=1).reshape(N_HEADS * PV_ROWS, tok).astype(BF16)
    k_ref[0] = k3
    v_ref[0] = v3
    k_tok = k3.reshape(D_MODEL, tok).T
    kb_ref[0] = k_tok.astype(BF16)
    for j in range(tok // BLOCK):
        vt_ref[0, j] = vt[:, j * BLOCK:(j + 1) * BLOCK]
        mean_ref[0, j] = jnp.mean(k_tok[j * BLOCK:(j + 1) * BLOCK], axis=0, keepdims=True)


def _qkv_prompt(h, g_q_in, g_kv, w_q, w_k, w_v, g_q, g_k):
    b, s, _ = h.shape
    nb = s // BLOCK
    col = lambda g: g[:, None]
    args = [col(g_q_in), col(g_kv), w_q, w_k, w_v, col(g_q), col(g_k)]
    specs = ([_const_spec((D_MODEL, 1))] * 2 + [_const_spec((D_MODEL, D_MODEL))] * 3
             + [_const_spec((HEAD_DIM, 1))] * 2)
    nj = QKV_BLOCKS_PER_STEP if nb % QKV_BLOCKS_PER_STEP == 0 else 1
    tok = nj * BLOCK
    head_spec = pl.BlockSpec((1, N_HEADS, HEAD_DIM, tok), lambda bi, i: (bi, 0, 0, i))
    head_shape = jax.ShapeDtypeStruct((b, N_HEADS, HEAD_DIM, s), F32)
    return pl.pallas_call(
        _qkv_prompt_body,
        grid=(b, nb // nj),
        in_specs=[pl.BlockSpec((1, tok, D_MODEL), lambda bi, i: (bi, i, 0))] + specs,
        out_specs=(pl.BlockSpec((1, D_MODEL, tok), lambda bi, i: (bi, 0, i)),
                   pl.BlockSpec((1, nj, N_HEADS * PV_ROWS, BLOCK), lambda bi, i: (bi, i, 0, 0)),
                   pl.BlockSpec((1, tok, D_MODEL), lambda bi, i: (bi, i, 0)),
                   head_spec, head_spec,
                   pl.BlockSpec((1, nj, 1, D_MODEL), lambda bi, i: (bi, i, 0, 0))),
        out_shape=(jax.ShapeDtypeStruct((b, D_MODEL, s), BF16),
                   jax.ShapeDtypeStruct((b, nb, N_HEADS * PV_ROWS, BLOCK), BF16),
                   jax.ShapeDtypeStruct((b, s, D_MODEL), BF16),
                   head_shape, head_shape,
                   jax.ShapeDtypeStruct((b, nb, 1, D_MODEL), F32)),
        compiler_params=_params(("parallel", "parallel")),
        name="qkv",
    )(h, *args)


def _qkv_decode_body(h_ref, gq_ref, gkv_ref, wq_ref, wk_ref, wv_ref, gqh_ref, gkh_ref, grp_ref,
                     q_ref, k_ref, v_ref):
    n = _rms(h_ref[...])
    hq = (n * gq_ref[...]).astype(BF16)
    c = (n * gkv_ref[...]).astype(BF16)

    def head_norm(y, g_ref):
        pieces = []
        for cb in range(D_MODEL // NORM_COLS):
            ys = y[:, cb * NORM_COLS:(cb + 1) * NORM_COLS]
            ms = _dot((ys * ys).astype(BF16), grp_ref[...])
            pieces.append(ys * lax.rsqrt(ms + EPS))
        return jnp.concatenate(pieces, axis=1) * g_ref[...]

    def project(x, w_ref):
        return lax.dot_general(x, w_ref[...], (((1,), (1,)), ((), ())), preferred_element_type=F32)

    q_ref[...] = head_norm(project(hq, wq_ref), gqh_ref)
    k_ref[...] = head_norm(project(c, wk_ref), gkh_ref)
    v_ref[...] = project(c, wv_ref)


def _qkv_decode(h, g_q_in, g_kv, w_q, w_k, w_v, g_q, g_k):
    m = h.shape[0]
    head = np.arange(NORM_COLS) // HEAD_DIM
    grp = jnp.asarray((head[:, None] == head[None, :]).astype(np.float32) / HEAD_DIM, dtype=BF16)
    args = [g_q_in[None, :], g_kv[None, :], w_q, w_k, w_v,
            jnp.tile(g_q, N_HEADS)[None, :], jnp.tile(g_k, N_HEADS)[None, :], grp]
    specs = ([_const_spec((1, D_MODEL))] * 2 + [_const_spec((D_MODEL, D_MODEL))] * 3
             + [_const_spec((1, D_MODEL))] * 2 + [_const_spec((NORM_COLS, NORM_COLS))])
    row_spec = pl.BlockSpec((m, D_MODEL), lambda i: (0, 0))
    shape = jax.ShapeDtypeStruct((m, D_MODEL), F32)
    return pl.pallas_call(
        _qkv_decode_body,
        grid=(1,),
        in_specs=[row_spec] + specs,
        out_specs=(row_spec, row_spec, row_spec),
        out_shape=(shape, shape, shape),
        compiler_params=_params(("arbitrary",)),
        name="qkv_decode",
    )(h, *args)


def _bucket_of_distance(n):
    n = np.asarray(n, dtype=np.int64)
    max_exact = N_BUCKETS // 2
    nf = np.maximum(n, 1).astype(np.float32)
    large = max_exact + (np.log(nf / np.float32(max_exact)) / np.float32(math.log(MAX_DISTANCE / max_exact))
                         * np.float32(N_BUCKETS - max_exact)).astype(np.int32)
    large = np.minimum(large, N_BUCKETS - 1)
    return np.where(n < max_exact, n, large).astype(np.int32)


def _bias_body(tab_ref, bm_ref, bv_ref, mat_ref, vec_ref):
    h = pl.program_id(0)
    mat = jnp.full(bm_ref.shape, NEG, F32)
    vec = jnp.full(bv_ref.shape, NEG, F32)
    for bkt in range(N_BUCKETS):
        val = tab_ref[bkt, h]
        mat = jnp.where(bm_ref[...] == bkt, val * LOG2E, mat)
        vec = jnp.where(bv_ref[...] == bkt, val, vec)
    mat_ref[0, 0] = jnp.full(bm_ref.shape[1:], tab_ref[FAR_BUCKET, h] * LOG2E, F32)
    mat_ref[0, 1:] = mat
    vec_ref[0] = vec


def _bias_tables(rel_table):
    key = np.arange(BLOCK)[:, None]
    qry = np.arange(BLOCK)[None, :]
    own = np.where(qry >= key, _bucket_of_distance(qry - key), -1)
    adj = _bucket_of_distance(BLOCK + qry - key)
    assert _bucket_of_distance(BLOCK + 1) == FAR_BUCKET
    bm = jnp.asarray(np.stack([adj, own]).astype(np.int32))
    bv = jnp.asarray(_bucket_of_distance(BLOCK - np.arange(BLOCK))[None, :].astype(np.int32))
    return pl.pallas_call(
        _bias_body,
        grid=(N_HEADS,),
        in_specs=[pl.BlockSpec(memory_space=pltpu.SMEM), _const_spec((2, BLOCK, BLOCK)), _const_spec((1, BLOCK))],
        out_specs=(pl.BlockSpec((1, 3, BLOCK, BLOCK), lambda h: (h, 0, 0, 0)),
                   pl.BlockSpec((1, 1, BLOCK), lambda h: (h, 0, 0))),
        out_shape=(jax.ShapeDtypeStruct((N_HEADS, 3, BLOCK, BLOCK), F32),
                   jax.ShapeDtypeStruct((N_HEADS, 1, BLOCK), F32)),
        compiler_params=_params(("arbitrary",)),
        name="rel_bias",
    )(rel_table, bm, bv)


def _top_blocks_penalty(gate, n_valid, own):
    nb = gate.shape[0]
    blk = lax.broadcasted_iota(jnp.int32, gate.shape, 0).astype(F32)
    valid = blk < n_valid
    g = jnp.where(valid, gate, NEG)
    sel = jnp.zeros(gate.shape, jnp.bool_)
    for _ in range(TOPK):
        mx = jnp.max(g, axis=0, keepdims=True)
        first = jnp.min(jnp.where(g == mx, blk, float(nb)), axis=0, keepdims=True)
        pick = blk == first
        sel = jnp.logical_or(sel, pick)
        g = jnp.where(pick, -jnp.inf, g)
    sel = jnp.logical_or(jnp.logical_and(sel, valid), blk == own)
    return jnp.where(sel, 0.0, NEG)


def _moba_body(qt_ref, k_ref, vt_ref, mean_ref, bias_ref, o_ref, qm_ref, pen_ref, sa_ref, sb_ref):
    i = pl.program_id(2)
    heads = range(MOBA_HEADS)
    subs = range(KEY_BLOCKS_PER_STEP)

    def lanes(t):
        lt = t // HEADS_PER_LANE_TILE
        return slice(lt * V7X_LANES, (lt + 1) * V7X_LANES)

    feat = lax.broadcasted_iota(jnp.int32, (V7X_LANES, BLOCK), 0)
    for t in heads:
        half = t % HEADS_PER_LANE_TILE
        in_head = jnp.logical_and(feat >= half * HEAD_DIM, feat < (half + 1) * HEAD_DIM)
        qt = qt_ref[0, lanes(t), :]
        qm_ref[t] = jnp.where(in_head, qt, jnp.zeros_like(qt))

    def rows(j):
        return pl.ds(pl.multiple_of(j * BLOCK, BLOCK), BLOCK)

    def head_rows(t):
        return slice(t * PV_ROWS, (t + 1) * PV_ROWS)

    def col_max(x):
        return jnp.max(x, axis=0, keepdims=True)


    def score_near(buf):
        j_adj = jnp.maximum(i - 1, 0)
        no_adj = jnp.where(i >= 1, 0.0, NEG)
        s_adj = [_dot(k_ref[0, rows(j_adj), lanes(t)], qm_ref[t]) for t in heads]
        s_own = [_dot(k_ref[0, rows(i), lanes(t)], qm_ref[t]) for t in heads]
        i_f = i.astype(F32)
        for t in heads:
            mean_hi, mean_lo = _split_bf16(mean_ref[0, :, lanes(t)])
            gate = _dot(mean_hi, qm_ref[t]) + _dot(mean_lo, qm_ref[t])
            pen_ref[t] = _top_blocks_penalty(gate, i_f, i_f)
        meta = []
        for t in heads:
            tiles = [s_adj[t] + bias_ref[t, 1], s_own[t] + bias_ref[t, 2]]
            offs = [pen_ref[t, pl.ds(j_adj, 1), :] + no_adj, jnp.zeros((1, BLOCK), F32)]
            for sub in subs:
                buf[t, sub] = tiles[sub]
            meta.append((offs, jnp.maximum(col_max(tiles[0]) + offs[0], col_max(tiles[1]))))
        return tuple(meta), (j_adj, i)

    n_far = jnp.maximum(i - 1, 0)

    def score_far(c, buf):
        key_rows = pl.ds(pl.multiple_of(c * KEY_ROWS_PER_STEP, KEY_ROWS_PER_STEP), KEY_ROWS_PER_STEP)
        blocks = tuple(c * KEY_BLOCKS_PER_STEP + sub for sub in subs)
        scores = [_dot(k_ref[0, key_rows, lanes(t)], qm_ref[t]) for t in heads]
        meta = []
        for t in heads:
            far_bias = bias_ref[t, 0, 0:1, 0:1]
            tiles = [scores[t][sub * BLOCK:(sub + 1) * BLOCK] for sub in subs]
            offs = [pen_ref[t, pl.ds(blocks[sub], 1), :] + far_bias + jnp.where(blocks[sub] < n_far, 0.0, NEG)
                    for sub in subs]
            cm = col_max(tiles[0]) + offs[0]
            for sub in subs[1:]:
                cm = jnp.maximum(cm, col_max(tiles[sub]) + offs[sub])
            for sub in subs:
                buf[t, sub] = tiles[sub]
            meta.append((offs, cm))
        return tuple(meta), blocks

    def fold(state, buf, meta, blocks):
        new = []
        for t in heads:
            m, acc = state[t]
            offs, cm = meta[t]
            m_new = jnp.maximum(m, cm)
            acc = jnp.exp2(m - m_new) * acc
            for sub in subs:
                p = jnp.exp2(buf[t, sub] - (m_new - offs[sub]))
                acc = acc + _dot(vt_ref[0, blocks[sub], head_rows(t), :], p.astype(BF16))
            new.append((m_new, acc))
        return tuple(new)

    def step(c2, carry):
        state, meta, blocks = carry
        meta_b, blocks_b = score_far(2 * c2, sb_ref)
        state = fold(state, sa_ref, meta, blocks)
        meta_a, blocks_a = score_far(2 * c2 + 1, sa_ref)
        state = fold(state, sb_ref, meta_b, blocks_b)
        return state, meta_a, blocks_a

    state = tuple((jnp.full((1, BLOCK), -jnp.inf, F32), jnp.zeros((PV_ROWS, BLOCK), F32)) for _ in heads)
    n_units = (n_far + KEY_BLOCKS_PER_STEP - 1) // KEY_BLOCKS_PER_STEP
    meta, blocks = score_near(sa_ref)
    state, meta, blocks = lax.fori_loop(0, n_units // 2, step, (state, meta, blocks))

    def last_with_odd_unit(args):
        state, meta, blocks = args
        meta_b, blocks_b = score_far(n_units - 1, sb_ref)
        return fold(fold(state, sa_ref, meta, blocks), sb_ref, meta_b, blocks_b)

    def last(args):
        state, meta, blocks = args
        return fold(state, sa_ref, meta, blocks)

    state = lax.cond(n_units % 2 == 1, last_with_odd_unit, last, (state, meta, blocks))
    ot = jnp.concatenate([acc[:HEAD_DIM] / acc[HEAD_DIM:HEAD_DIM + 1] for (_, acc) in state], axis=0)
    o_ref[0] = ot.T.astype(o_ref.dtype)


def _moba_prompt(qt, kb, vt, means, bias_mat):
    b, s, _ = kb.shape
    nb = s // BLOCK
    assert pl.cdiv(max(nb - 2, 0), KEY_BLOCKS_PER_STEP) * KEY_BLOCKS_PER_STEP <= nb
    width = MOBA_HEADS * HEAD_DIM
    return pl.pallas_call(
        _moba_body,
        grid=(b, D_MODEL // width, nb),
        in_specs=[pl.BlockSpec((1, width, BLOCK), lambda bi, p, i: (bi, p, i)),
                  pl.BlockSpec((1, s, width), lambda bi, p, i: (bi, 0, p)),
                  pl.BlockSpec((1, nb, MOBA_HEADS * PV_ROWS, BLOCK), lambda bi, p, i: (bi, 0, p, 0)),
                  pl.BlockSpec((1, nb, width), lambda bi, p, i: (bi, 0, p)),
                  pl.BlockSpec((MOBA_HEADS, 3, BLOCK, BLOCK), lambda bi, p, i: (p, 0, 0, 0))],
        out_specs=pl.BlockSpec((1, BLOCK, width), lambda bi, p, i: (bi, i, p)),
        out_shape=jax.ShapeDtypeStruct((b, s, D_MODEL), BF16),
        scratch_shapes=[pltpu.VMEM((MOBA_HEADS, V7X_LANES, BLOCK), BF16),
                        pltpu.VMEM((MOBA_HEADS, nb, BLOCK), F32)]
        + [pltpu.VMEM((MOBA_HEADS, KEY_BLOCKS_PER_STEP, BLOCK, BLOCK), F32)] * 2,
        compiler_params=_params(("parallel", "parallel", "arbitrary")),
        name="moba_prompt",
    )(qt, kb, vt, means, bias_mat)


def _page_mean_body(pt_ref, *refs):
    del pt_ref
    pages, o_ref = refs[:-1], refs[-1]
    ones = jnp.ones((8, PAGE_SIZE), BF16)
    contract_minor = (((1,), (1,)), ((), ()))
    rows = []
    for t in range(len(pages) // PAGES_PER_BLOCK):
        tot = pages[PAGES_PER_BLOCK * t][0]
        for p in range(1, PAGES_PER_BLOCK):
            tot = tot + pages[PAGES_PER_BLOCK * t + p][0]
        hi, lo = _split_bf16(tot.reshape(D_MODEL, PAGE_SIZE))
        sums = (lax.dot_general(ones, hi, contract_minor, preferred_element_type=F32)
                + lax.dot_general(ones, lo, contract_minor, preferred_element_type=F32))
        rows.append(sums[0:1])
    o_ref[0] = jnp.concatenate(rows, axis=0) * (1.0 / BLOCK)


def _cached_block_means(cache_kt, page_table):
    b, n_pages = page_table.shape
    n_blocks = n_pages // PAGES_PER_BLOCK
    assert n_blocks % MEAN_BLOCKS_PER_STEP == 0
    steps = n_blocks // MEAN_BLOCKS_PER_STEP
    pages_per_step = MEAN_BLOCKS_PER_STEP * PAGES_PER_BLOCK

    def page_spec(t):
        return pl.BlockSpec((1, N_HEADS, HEAD_DIM, PAGE_SIZE),
                            lambda bi, g, pt: (pt[bi, g * pages_per_step + t], 0, 0, 0))

    return pl.pallas_call(
        _page_mean_body,
        grid_spec=pltpu.PrefetchScalarGridSpec(
            num_scalar_prefetch=1,
            grid=(b, steps),
            in_specs=[page_spec(t) for t in range(pages_per_step)],
            out_specs=pl.BlockSpec((1, MEAN_BLOCKS_PER_STEP, D_MODEL), lambda bi, g, pt: (bi, g, 0)),
        ),
        out_shape=jax.ShapeDtypeStruct((b, n_blocks, D_MODEL), F32),
        compiler_params=_params(("parallel", "parallel")),
        name="page_means",
    )(page_table, *([cache_kt] * pages_per_step))


def _decode_topk_body(q_ref, cm_ref, seg_ref, idx_ref):
    n_blocks = cm_ref.shape[1]
    blk = lax.broadcasted_iota(jnp.int32, (n_blocks, N_HEADS), 0).astype(F32)
    for r in range(cm_ref.shape[0]):
        hi, lo = _split_bf16(cm_ref[r] * q_ref[r])
        gate = _dot(hi, seg_ref[...]) + _dot(lo, seg_ref[...])
        rows = []
        for _ in range(TOPK):
            mx = jnp.max(gate, axis=0, keepdims=True)
            first = jnp.min(jnp.where(gate == mx, blk, float(n_blocks)), axis=0, keepdims=True)
            rows.append(first)
            gate = jnp.where(blk == first, -jnp.inf, gate)
        idx_ref[r] = jnp.concatenate(rows, axis=0).astype(jnp.int32)


def _decode_topk(q, cmeans):
    b, n_blocks, _ = cmeans.shape
    rows = _pick_tile(b, TOPK_ROWS_PER_STEP)
    seg = jnp.asarray((np.arange(D_MODEL)[:, None] // HEAD_DIM == np.arange(N_HEADS)[None, :]).astype(np.float32),
                      dtype=BF16)
    return pl.pallas_call(
        _decode_topk_body,
        grid=(b // rows,),
        in_specs=[pl.BlockSpec((rows, 1, D_MODEL), lambda bi: (bi, 0, 0)),
                  pl.BlockSpec((rows, n_blocks, D_MODEL), lambda bi: (bi, 0, 0)),
                  _const_spec((D_MODEL, N_HEADS))],
        out_specs=pl.BlockSpec((rows, TOPK, N_HEADS), lambda bi: (bi, 0, 0)),
        out_shape=jax.ShapeDtypeStruct((b, TOPK, N_HEADS), jnp.int32),
        compiler_params=_params(("parallel",)),
        name="decode_topk",
    )(q, cmeans, seg)


def _decode_attn_body(pt_ref, idx_ref, tab_ref, q_ref, kn_ref, vn_ref, near_ref, ck_ref, cv_ref, o_ref,
                      kbuf, vbuf, sem, *, n_blocks, n_pages):
    bi = pl.program_id(0)

    def tile(h, r, p):
        return (h * TOPK + r) * PAGES_PER_BLOCK + p

    def gather(row, slot, action):
        def per_head(h, carry):
            for r in range(TOPK):
                blk = idx_ref[row, r * N_HEADS + h]
                for p in range(PAGES_PER_BLOCK):
                    logical = jnp.minimum(blk * PAGES_PER_BLOCK + p, n_pages - 1)
                    phys = pt_ref[row, logical]
                    n = tile(h, r, p)
                    action(pltpu.make_async_copy(ck_ref.at[phys, h], kbuf.at[slot, n], sem.at[0, slot]))
                    action(pltpu.make_async_copy(cv_ref.at[phys, h], vbuf.at[slot, n], sem.at[1, slot]))
            return carry
        lax.fori_loop(0, N_HEADS, per_head, 0)

    slot = bi % 2

    @pl.when(bi == 0)
    def _():
        gather(bi, slot, lambda c: c.start())

    @pl.when(bi + 1 < pl.num_programs(0))
    def _():
        gather(bi + 1, 1 - slot, lambda c: c.start())

    gather(bi, slot, lambda c: c.wait())

    heads = range(N_HEADS)
    sel = [(r, p) for r in range(TOPK) for p in range(PAGES_PER_BLOCK)]
    q = [q_ref[0, :, h:h + 1] for h in heads]
    s_own = [jnp.sum(q[h] * kn_ref[0, :, h:h + 1], axis=0, keepdims=True) * SCALE + tab_ref[0, h] for h in heads]
    scores = []
    for h in heads:
        far = tab_ref[N_BUCKETS - 1, h]
        row = []
        for r, p in sel:
            blk = idx_ref[bi, r * N_HEADS + h]
            s = jnp.sum(kbuf[slot, tile(h, r, p)] * q[h], axis=0, keepdims=True) * SCALE
            bias = jnp.where(blk == n_blocks - 1, near_ref[h, :, p * PAGE_SIZE:(p + 1) * PAGE_SIZE], far)
            row.append(jnp.where(blk < n_blocks, s + bias, NEG))
        scores.append(row)
    m = []
    for h in heads:
        mh = scores[h][0]
        for s in scores[h][1:]:
            mh = jnp.maximum(mh, s)
        m.append(jnp.maximum(s_own[h], jnp.max(mh, axis=1, keepdims=True)))
    probs = [[jnp.exp(s - m[h]) for s in scores[h]] for h in heads]
    for h in heads:
        p_own = jnp.exp(s_own[h] - m[h])
        ps = probs[h][0]
        for p in probs[h][1:]:
            ps = ps + p
        l = p_own + jnp.sum(ps, axis=1, keepdims=True)
        acc = vbuf[slot, tile(h, 0, 0)] * probs[h][0]
        for n in range(1, len(sel)):
            acc = acc + vbuf[slot, tile(h, *sel[n])] * probs[h][n]
        o_ref[0, :, h:h + 1] = (p_own * vn_ref[0, :, h:h + 1] + jnp.sum(acc, axis=1, keepdims=True)) / l


def _decode_attn(q, k_new, v_new, cache_kt, cache_vt, page_table, idx, rel_table, near_bias):
    b, n_pages = page_table.shape
    n_blocks = n_pages // PAGES_PER_BLOCK
    n_tiles = N_HEADS * TOPK * PAGES_PER_BLOCK
    tok_spec = pl.BlockSpec((1, HEAD_DIM, N_HEADS), lambda bi, pt, ix: (bi, 0, 0))
    return pl.pallas_call(
        functools.partial(_decode_attn_body, n_blocks=n_blocks, n_pages=n_pages),
        grid_spec=pltpu.PrefetchScalarGridSpec(
            num_scalar_prefetch=2,
            grid=(b,),
            in_specs=[pl.BlockSpec(memory_space=pltpu.SMEM), tok_spec, tok_spec, tok_spec,
                      pl.BlockSpec((N_HEADS, 1, BLOCK), lambda bi, pt, ix: (0, 0, 0)),
                      pl.BlockSpec(memory_space=pl.ANY), pl.BlockSpec(memory_space=pl.ANY)],
            out_specs=tok_spec,
            scratch_shapes=[pltpu.VMEM((2, n_tiles, HEAD_DIM, PAGE_SIZE), F32),
                            pltpu.VMEM((2, n_tiles, HEAD_DIM, PAGE_SIZE), F32),
                            pltpu.SemaphoreType.DMA((2, 2))],
        ),
        out_shape=jax.ShapeDtypeStruct(q.shape, F32),
        compiler_params=_params(("arbitrary",)),
        name="decode_attn",
    )(page_table, idx, rel_table, q, k_new, v_new, near_bias, cache_kt, cache_vt)


def _pick_tile(m, target):
    return target if m % target == 0 else m


def kernel(x_prompt, x_sample, cache_k, cache_v, page_table, rel_table, g_mix, g_ffn, w_gin, g_gv, w_sp, b_sp,
           w_gout, g_kv, w_k, w_v, g_k, w_q, g_q, w_o, w_f1, w_f3, w_f2):
    bsz, seq, _ = x_prompt.shape
    dec = x_sample.shape[0]
    assert page_table.shape[1] % PAGES_PER_BLOCK == 0 and seq % BLOCK == 0
    cast = lambda w: w.astype(BF16)
    w_gin_b, w_gout_b, w_o_b = cast(w_gin[0]), cast(w_gout[0]), cast(w_o[0])
    ffn_w = (cast(w_f1), cast(w_f3), cast(w_f2))
    qkv_w = (g_mix[1], g_kv, cast(w_q[0].T), cast(w_k.T), cast(w_v.T), g_q[0], g_k)

    bias_mat, bias_near = _bias_tables(rel_table)

    xp = x_prompt.reshape(bsz * seq, D_MODEL)
    tm = _pick_tile(bsz * seq, 512)
    h = _gmlp(xp, g_mix[0], w_gin_b, g_gv[0], w_sp[0], b_sp[0], w_gout_b, single_token=False, tm=tm)
    h = _ffn(h, g_ffn[0], *ffn_w, 0, tm=tm)
    qt, vt, kb, prompt_kt, prompt_vt, means = _qkv_prompt(h.reshape(bsz, seq, D_MODEL), *qkv_w)
    attn = _moba_prompt(qt, kb, vt, means.reshape(bsz, seq // BLOCK, D_MODEL), bias_mat)
    y_prompt = _ffn(h, g_ffn[1], *ffn_w, 1, tm=tm,
                    attn=attn.reshape(bsz * seq, D_MODEL), w_o=w_o_b).reshape(bsz, seq, D_MODEL)

    cache_kt = cache_k.transpose(0, 1, 3, 2)
    cache_vt = cache_v.transpose(0, 1, 3, 2)
    xs = x_sample.reshape(dec, D_MODEL)
    hs, v_rows = _gmlp(xs, g_mix[0], w_gin_b, g_gv[0], w_sp[0], b_sp[0], w_gout_b, single_token=True, tm=dec)
    hs = _ffn(hs, g_ffn[0], *ffn_w, 0, tm=dec)
    qs, ks, vs = _qkv_decode(hs, *qkv_w)
    cmeans = _cached_block_means(cache_kt, page_table)
    idx = _decode_topk(qs.reshape(dec, 1, D_MODEL), cmeans).reshape(dec, TOPK * N_HEADS)
    as_cols = lambda a: a.reshape(dec, N_HEADS, HEAD_DIM).transpose(0, 2, 1)
    attn_s = _decode_attn(as_cols(qs), as_cols(ks), as_cols(vs), cache_kt, cache_vt, page_table, idx,
                          rel_table, bias_near)
    attn_s = attn_s.transpose(0, 2, 1).reshape(dec, D_MODEL).astype(BF16)
    y_sample = _ffn(hs, g_ffn[1], *ffn_w, 1, tm=dec,
                    attn=attn_s, w_o=w_o_b).reshape(dec, 1, D_MODEL)

    as_heads = lambda a: a.reshape(dec, N_HEADS, 1, HEAD_DIM)
    return (y_prompt, y_sample, prompt_kt.transpose(0, 1, 3, 2), prompt_vt.transpose(0, 1, 3, 2),
            as_heads(ks), as_heads(vs), v_rows.reshape(1, dec, 1, D_GATE))
```
